```python
import jax, jax.numpy as jnp
from jax import lax
import numpy as np

D_MODEL = 4096
BATCH = 1
SEQ = 16384
DEPTH = 4

N_MIXERS = 2
FOX_HEADS = 32
FOX_HEAD_DIM = D_MODEL // FOX_HEADS
Q_BLOCK = 128
HGRN_EXPAND = 128
HGRN_HEADS = D_MODEL // HGRN_EXPAND
HGRN_DK = HGRN_EXPAND
HGRN_DV = D_MODEL // HGRN_HEADS
HGRN_KEY = HGRN_HEADS * HGRN_DK
CHUNK = 64
D_FF = 4 * D_MODEL
N_FOX = (DEPTH + 1) // 2
N_HGRN = DEPTH // 2
FOX_IN = 4 * D_MODEL + FOX_HEADS
HGRN_IN = 2 * HGRN_KEY + 2 * D_MODEL
NORM_EPS = 1e-6

kernel_name = "fox_hgrn2_interleaved_hybrid"


def rms_norm(x, gain):
    xf = x.astype(jnp.float32)
    y = xf * lax.rsqrt(jnp.mean(xf * xf, axis=-1, keepdims=True) + NORM_EPS)
    return (y * gain.astype(jnp.float32)).astype(x.dtype)


def forgetting_attention(q, k, v, log_f):
    B, S, H, Dh = q.shape
    nb = S // Q_BLOCK
    cum = jnp.cumsum(log_f, axis=1).transpose(0, 2, 1)
    kT = k.transpose(0, 2, 1, 3)
    vT = v.transpose(0, 2, 1, 3)
    q_blocks = q.reshape(B, nb, Q_BLOCK, H, Dh).transpose(1, 0, 3, 2, 4)
    c_blocks = cum.reshape(B, H, nb, Q_BLOCK).transpose(2, 0, 1, 3)
    starts = jnp.arange(nb, dtype=jnp.int32) * Q_BLOCK
    k_pos = jnp.arange(S, dtype=jnp.int32)
    scale = Dh ** -0.5

    def one_block(args):
        qb, cb, start = args
        s = jnp.einsum('bhqd,bhkd->bhqk', qb, kT, preferred_element_type=jnp.float32) * scale
        s = s + (cb[..., :, None] - cum[:, :, None, :])
        q_pos = start + jnp.arange(Q_BLOCK, dtype=jnp.int32)
        causal = k_pos[None, :] <= q_pos[:, None]
        p = jax.nn.softmax(jnp.where(causal, s, -jnp.inf), axis=-1)
        return jnp.einsum('bhqk,bhkd->bhqd', p.astype(vT.dtype), vT)

    o = lax.map(one_block, (q_blocks, c_blocks, starts))
    return o.transpose(1, 0, 3, 2, 4).reshape(B, S, H, Dh)


def hgrn2_chunked_scan(q, k, v, log_f):
    B, S, H, Dk = q.shape
    Dv = v.shape[-1]
    nc = S // CHUNK

    def to_chunks(t):
        return t.reshape(B, nc, CHUNK, H, t.shape[-1]).transpose(1, 0, 3, 2, 4)

    causal = jnp.tril(jnp.ones((CHUNK, CHUNK), dtype=bool))[:, :, None]

    def step(state, inp):
        qc, kc, vc, gc = inp
        qf, kf, vf = qc.astype(jnp.float32), kc.astype(jnp.float32), vc.astype(jnp.float32)
        b = jnp.cumsum(gc, axis=2)
        rel = b[:, :, :, None, :] - b[:, :, None, :, :]
        decay = jnp.exp(jnp.where(causal, rel, -jnp.inf))
        scores = jnp.einsum('bhtk,bhsk,bhtsk->bhts', qf, kf, decay)
        o = (jnp.einsum('bhts,bhsv->bhtv', scores, vf)
             + jnp.einsum('bhtk,bhkv->bhtv', qf * jnp.exp(b), state))
        b_last = b[:, :, -1:, :]
        state = (jnp.exp(b_last)[:, :, 0, :, None] * state
                 + jnp.einsum('bhsk,bhsv->bhkv', kf * jnp.exp(b_last - b), vf))
        return state, o

    state0 = jnp.zeros((B, H, Dk, Dv), jnp.float32)
    _, o = lax.scan(step, state0, (to_chunks(q), to_chunks(k), to_chunks(v), to_chunks(log_f)))
    return o.transpose(1, 0, 3, 2, 4).reshape(B, S, H, Dv)


def fox_mixer(h, w_in, w_out, q_gain, k_gain, fgate_bias):
    B, S, _ = h.shape
    proj = h @ w_in
    q, k, v, gate, fz = jnp.split(proj, [D_MODEL, 2 * D_MODEL, 3 * D_MODEL, 4 * D_MODEL], axis=-1)
    shp = (B, S, FOX_HEADS, FOX_HEAD_DIM)
    q = rms_norm(q.reshape(shp), q_gain)
    k = rms_norm(k.reshape(shp), k_gain)
    v = v.reshape(shp)
    log_f = jax.nn.log_sigmoid(fz.astype(jnp.float32) + fgate_bias.astype(jnp.float32))
    o = forgetting_attention(q, k, v, log_f)
    o = o.reshape(B, S, D_MODEL) * jax.nn.sigmoid(gate)
    return o @ w_out


def hgrn2_mixer(h, w_in, w_out, out_gain, lower_bound):
    B, S, _ = h.shape
    proj = h @ w_in
    q, fz, i_in, gate = jnp.split(proj, [HGRN_KEY, 2 * HGRN_KEY, 2 * HGRN_KEY + D_MODEL], axis=-1)
    kshp = (B, S, HGRN_HEADS, HGRN_DK)
    q = jax.nn.silu(q).reshape(kshp)
    fgate = lower_bound + (1.0 - lower_bound) * jax.nn.sigmoid(fz.astype(jnp.float32))
    log_f = jnp.log(fgate).reshape(kshp)
    k = (1.0 - fgate).astype(h.dtype).reshape(kshp)
    v = i_in.reshape(B, S, HGRN_HEADS, HGRN_DV)
    o = hgrn2_chunked_scan(q, k, v, log_f)
    o = rms_norm(o.astype(h.dtype), out_gain).reshape(B, S, D_MODEL) * jax.nn.silu(gate)
    return o @ w_out


def sq_relu_mlp(h, w_up, w_down):
    return jnp.square(jax.nn.relu(h @ w_up)) @ w_down


def setup_inputs(seed: int = 0) -> dict:
    key = jax.random.key(seed)
    ks = jax.random.split(key, 16)
    f32 = jnp.float32

    def normal(k, shape, scale):
        return jax.random.normal(k, shape, f32) * scale

    def gain(k, shape):
        return 1.0 + 0.02 * jax.random.normal(k, shape, f32)

    return {
        "x": normal(ks[0], (BATCH, SEQ, D_MODEL), 1.0),
        "fox_w_in": normal(ks[1], (N_FOX, D_MODEL, FOX_IN), D_MODEL ** -0.5),
        "fox_w_out": normal(ks[2], (N_FOX, D_MODEL, D_MODEL), D_MODEL ** -0.5),
        "fox_q_gain": gain(ks[3], (N_FOX, FOX_HEAD_DIM)),
        "fox_k_gain": gain(ks[4], (N_FOX, FOX_HEAD_DIM)),
        "fox_fgate_bias": jax.random.uniform(ks[5], (N_FOX, FOX_HEADS), f32, 1.0, 4.0),
        "hgrn_w_in": normal(ks[6], (N_HGRN, D_MODEL, HGRN_IN), D_MODEL ** -0.5),
        "hgrn_w_out": normal(ks[7], (N_HGRN, D_MODEL, D_MODEL), D_MODEL ** -0.5),
        "hgrn_out_gain": gain(ks[8], (N_HGRN, HGRN_DV)),
        "hgrn_lb_logits": normal(ks[9], (DEPTH, HGRN_KEY), 0.1),
        "mixer_norm_gain": gain(ks[10], (DEPTH, D_MODEL)),
        "mlp_norm_gain": gain(ks[11], (DEPTH, D_MODEL)),
        "mlp_w_up": normal(ks[12], (DEPTH, D_MODEL, D_FF), D_MODEL ** -0.5),
        "mlp_w_down": normal(ks[13], (DEPTH, D_FF, D_MODEL), D_FF ** -0.5),
        "final_norm_gain": gain(ks[14], (D_MODEL,)),
    }


def reference(x, fox_w_in, fox_w_out, fox_q_gain, fox_k_gain, fox_fgate_bias,
              hgrn_w_in, hgrn_w_out, hgrn_out_gain, hgrn_lb_logits,
              mixer_norm_gain, mlp_norm_gain, mlp_w_up, mlp_w_down, final_norm_gain):
    lb_p = jax.nn.softmax(hgrn_lb_logits.astype(jnp.float32), axis=0)
    lower_bounds = jnp.cumsum(lb_p, axis=0) - lb_p[0]
    h = x
    for i in range(DEPTH):
        j = i // N_MIXERS
        n = rms_norm(h, mixer_norm_gain[i])
        if i % N_MIXERS == 0:
            mix = fox_mixer(n, fox_w_in[j], fox_w_out[j], fox_q_gain[j], fox_k_gain[j], fox_fgate_bias[j])
        else:
            mix = hgrn2_mixer(n, hgrn_w_in[j], hgrn_w_out[j], hgrn_out_gain[j], lower_bounds[i])
        h = h + mix.astype(h.dtype)
        h = h + sq_relu_mlp(rms_norm(h, mlp_norm_gain[i]), mlp_w_up[i], mlp_w_down[i]).astype(h.dtype)
    return rms_norm(h, final_norm_gain)
```

```python
import functools

import jax
import jax.numpy as jnp
from jax import lax
from jax.experimental import pallas as pl
from jax.experimental.pallas import tpu as pltpu

F32 = jnp.float32
BF16 = jnp.bfloat16

HEAD_DIM = 128
NORM_EPS = 1e-6
NEG_BIG = -1e30
VMEM_LIMIT_BYTES = 56 * 1024 * 1024

HGRN_CHUNK = 256
HGRN_DIAG = 8

_NT = (((1,), (1,)), ((), ()))
_TN = (((0,), (0,)), ((), ()))


def _tile(n, pref):
    t = min(n, pref)
    while n % t:
        t //= 2
    return t


def _params(*sem):
    return pltpu.CompilerParams(dimension_semantics=sem, vmem_limit_bytes=VMEM_LIMIT_BYTES)


def _split_bf16(a, terms):
    parts = []
    r = a
    for _ in range(terms):
        p = r.astype(BF16)
        parts.append(p)
        r = r - p.astype(F32)
    return parts


def _sigmoid(x):
    return 1.0 / (1.0 + jnp.exp(-x))


def _log_sigmoid(x):
    return jnp.minimum(x, 0.0) - jnp.log(1.0 + jnp.exp(-jnp.abs(x)))


def _rmsnorm_kernel(x_ref, g_ref, o_ref):
    x = x_ref[...]
    ms = jnp.mean(x * x, axis=-1, keepdims=True)
    o_ref[...] = (x * lax.rsqrt(ms + NORM_EPS) * g_ref[...]).astype(o_ref.dtype)


def _rmsnorm(x, gain, out_dtype):
    s, d = x.shape
    tm = _tile(s, 256)
    return pl.pallas_call(
        _rmsnorm_kernel,
        out_shape=jax.ShapeDtypeStruct((s, d), out_dtype),
        grid=(s // tm,),
        in_specs=[pl.BlockSpec((tm, d), lambda i: (i, 0)),
                  pl.BlockSpec((1, d), lambda i: (0, 0))],
        out_specs=pl.BlockSpec((tm, d), lambda i: (i, 0)),
        compiler_params=_params("parallel"),
        name="rmsnorm",
    )(x, gain.reshape(1, d).astype(F32))


def _matmul_kernel(x_ref, w_ref, *refs, nk, n_aux, n_out, epilogue):
    aux = refs[:n_aux]
    outs = refs[n_aux:n_aux + n_out]
    part = jnp.dot(x_ref[...], w_ref[...], preferred_element_type=F32)
    if nk == 1:
        epilogue(part, aux, outs)
        return
    k = pl.program_id(2)
    acc_ref = outs[0]

    @pl.when(k == 0)
    def _():
        acc_ref[...] = part

    @pl.when(jnp.logical_and(k > 0, k < nk - 1))
    def _():
        acc_ref[...] += part

    @pl.when(k == nk - 1)
    def _():
        epilogue(acc_ref[...] + part, aux, outs)


def _matmul(x, w, *, tm, tn, tk, n_cols, w_col_block, epilogue, aux, aux_specs, out_shapes, name):
    m, kdim = x.shape
    nk = kdim // tk
    if nk > 1:
        assert len(out_shapes) == 1 and out_shapes[0].dtype == F32
    kern = functools.partial(_matmul_kernel, nk=nk, n_aux=len(aux), n_out=len(out_shapes),
                             epilogue=epilogue)
    out = pl.pallas_call(
        kern,
        out_shape=out_shapes,
        grid=(m // tm, n_cols // tn, nk),
        in_specs=[pl.BlockSpec((tm, tk), lambda i, j, k: (i, k)),
                  pl.BlockSpec((tk, tn), lambda i, j, k: (k, w_col_block(j)))] + list(aux_specs),
        out_specs=[pl.BlockSpec((tm, tn), lambda i, j, k: (i, j)) for _ in out_shapes],
        compiler_params=_params("parallel", "parallel", "arbitrary"),
        name=name,
    )(x, w, *aux)
    return out


def _head_rmsnorm_store(acc, gain, scale, o_ref):
    for c in range(acc.shape[1] // HEAD_DIM):
        sl = slice(c * HEAD_DIM, (c + 1) * HEAD_DIM)
        xc = acc[:, sl]
        ms = jnp.mean(xc * xc, axis=-1, keepdims=True)
        y = xc * lax.rsqrt(ms + NORM_EPS) * gain
        if scale != 1.0:
            y = y * scale
        o_ref[:, sl] = y.astype(o_ref.dtype)


def _fox_in_proj(n, w_qkvg, q_gain, k_gain):
    s, d = n.shape
    tm, tn = _tile(s, 1024), _tile(d, 512)
    nq = d // tn
    scale = HEAD_DIM ** -0.5

    def epilogue(acc, aux, outs):
        qg_ref, kg_ref = aux
        (o_ref,) = outs
        j = pl.program_id(1)

        @pl.when(j < nq)
        def _():
            _head_rmsnorm_store(acc, qg_ref[...], scale, o_ref)

        @pl.when(jnp.logical_and(j >= nq, j < 2 * nq))
        def _():
            _head_rmsnorm_store(acc, kg_ref[...], 1.0, o_ref)

        @pl.when(jnp.logical_and(j >= 2 * nq, j < 3 * nq))
        def _():
            o_ref[...] = acc.astype(o_ref.dtype)

        @pl.when(j >= 3 * nq)
        def _():
            o_ref[...] = _sigmoid(acc).astype(o_ref.dtype)

    gspec = pl.BlockSpec((1, HEAD_DIM), lambda i, j, k: (0, 0))
    (out,) = _matmul(n, w_qkvg, tm=tm, tn=tn, tk=d, n_cols=4 * d, w_col_block=lambda j: j,
                     epilogue=epilogue,
                     aux=(q_gain.reshape(1, HEAD_DIM).astype(F32), k_gain.reshape(1, HEAD_DIM).astype(F32)),
                     aux_specs=(gspec, gspec),
                     out_shapes=[jax.ShapeDtypeStruct((s, 4 * d), BF16)], name="fox_in_proj")
    return out


def _hgrn_in_proj_qvg(n, w_in):
    s, d = n.shape
    tm, tn = _tile(s, 1024), _tile(d, 512)
    nq = d // tn

    def epilogue(acc, aux, outs):
        (o_ref,) = outs
        j = pl.program_id(1)
        is_v = jnp.logical_and(j >= nq, j < 2 * nq)

        @pl.when(is_v)
        def _():
            o_ref[...] = acc.astype(o_ref.dtype)

        @pl.when(jnp.logical_not(is_v))
        def _():
            o_ref[...] = (acc * _sigmoid(acc)).astype(o_ref.dtype)

    (out,) = _matmul(n, w_in, tm=tm, tn=tn, tk=d, n_cols=3 * d,
                     w_col_block=lambda j: jnp.where(j < nq, j, j + nq),
                     epilogue=epilogue, aux=(), aux_specs=(),
                     out_shapes=[jax.ShapeDtypeStruct((s, 3 * d), BF16)], name="hgrn_in_proj_qvg")
    return out


def _hgrn_in_proj_forget(n, w_in, lb_logits, layer):
    s, d = n.shape
    depth = lb_logits.shape[0]
    tm, tn = _tile(s, 1024), _tile(d, 512)
    nq = d // tn

    def epilogue(acc, aux, outs):
        (lb_ref,) = aux
        g_ref, k_ref = outs
        z = lb_ref[...]
        e = jnp.exp(z - jnp.max(z, axis=0, keepdims=True))
        p = e / jnp.sum(e, axis=0, keepdims=True)
        lb = jnp.zeros((1, tn), F32)
        for r in range(1, layer + 1):
            lb = lb + p[r:r + 1, :]
        f = lb + (1.0 - lb) * _sigmoid(acc)
        g_ref[...] = jnp.log(f)
        k_ref[...] = (1.0 - f).astype(k_ref.dtype)

    g, k = _matmul(n, w_in, tm=tm, tn=tn, tk=d, n_cols=d, w_col_block=lambda j: j + nq,
                   epilogue=epilogue, aux=(lb_logits.astype(F32),),
                   aux_specs=(pl.BlockSpec((depth, tn), lambda i, j, k: (0, j)),),
                   out_shapes=[jax.ShapeDtypeStruct((s, d), F32), jax.ShapeDtypeStruct((s, d), BF16)],
                   name="hgrn_in_proj_forget")
    return g, k


def _proj_residual(x, w, h, *, name):
    s, kdim = x.shape
    d = w.shape[1]
    tm, tn = _tile(s, 1024), _tile(d, 1024)
    tk = _tile(kdim, 4096 if kdim <= 4096 else 2048)
    if kdim // tk == 1:
        tn = _tile(d, 512)

    def epilogue(acc, aux, outs):
        (h_ref,) = aux
        (o_ref,) = outs
        o_ref[...] = h_ref[...] + acc

    (out,) = _matmul(x, w, tm=tm, tn=tn, tk=tk, n_cols=d, w_col_block=lambda j: j,
                     epilogue=epilogue, aux=(h,),
                     aux_specs=(pl.BlockSpec((tm, tn), lambda i, j, k: (i, j)),),
                     out_shapes=[jax.ShapeDtypeStruct((s, d), F32)], name=name)
    return out


def _mlp_up(n, w_up):
    s, d = n.shape
    f = w_up.shape[1]
    tm, tn = _tile(s, 1024), _tile(f, 512)

    def epilogue(acc, aux, outs):
        (o_ref,) = outs
        r = jnp.maximum(acc, 0.0)
        o_ref[...] = (r * r).astype(o_ref.dtype)

    (out,) = _matmul(n, w_up, tm=tm, tn=tn, tk=d, n_cols=f, w_col_block=lambda j: j,
                     epilogue=epilogue, aux=(), aux_specs=(),
                     out_shapes=[jax.ShapeDtypeStruct((s, f), BF16)], name="mlp_up")
    return out


def _fox_forget_kernel(x_ref, wft_ref, bias_ref, tri_ref, c_ref, carry_ref):
    i = pl.program_id(0)

    @pl.when(i == 0)
    def _():
        carry_ref[...] = jnp.zeros_like(carry_ref)

    fz = lax.dot_general(wft_ref[...], x_ref[...], _NT, preferred_element_type=F32)
    lf = _log_sigmoid(fz + bias_ref[...])
    tri = tri_ref[...]
    cs = jnp.zeros_like(lf)
    for part in _split_bf16(lf, 3):
        cs = cs + jnp.dot(part, tri, preferred_element_type=F32)
    c = cs + carry_ref[...]
    c_ref[...] = c
    carry_ref[...] = c[:, c.shape[1] - 1:]


def _fox_forget_cumsum(n, w_f, bias):
    s, d = n.shape
    heads = w_f.shape[1]
    tm = _tile(s, 512)
    r = jnp.arange(tm)
    tri = (r[:, None] <= r[None, :]).astype(BF16)
    return pl.pallas_call(
        _fox_forget_kernel,
        out_shape=jax.ShapeDtypeStruct((heads, s), F32),
        grid=(s // tm,),
        in_specs=[pl.BlockSpec((tm, d), lambda i: (i, 0)),
                  pl.BlockSpec((heads, d), lambda i: (0, 0)),
                  pl.BlockSpec((heads, 1), lambda i: (0, 0)),
                  pl.BlockSpec((tm, tm), lambda i: (0, 0))],
        out_specs=pl.BlockSpec((heads, tm), lambda i: (0, i)),
        scratch_shapes=[pltpu.VMEM((heads, 1), F32)],
        compiler_params=_params("arbitrary"),
        name="fox_forget_cumsum",
    )(n, w_f.T, bias.reshape(heads, 1).astype(F32), tri)


def _fox_attn_kernel(q_ref, k_ref, v_ref, c_ref, gate_ref, o_ref, *, tq):
    i = pl.program_id(1)
    q = q_ref[...]
    row0 = pl.multiple_of(i * tq, tq)
    c0 = c_ref[:, pl.ds(row0, HEAD_DIM)][:, 0:1]

    def chunk(j, carry, masked):
        m, l, acc = carry
        col0 = pl.multiple_of(j * tq, tq)
        kc = k_ref[pl.ds(col0, tq), :]
        vc = v_ref[pl.ds(col0, tq), :]
        s = lax.dot_general(q, kc, _NT, preferred_element_type=F32)
        s = s + (c0 - c_ref[:, pl.ds(col0, tq)])
        if masked:
            rows = lax.broadcasted_iota(jnp.int32, (tq, tq), 0)
            cols = lax.broadcasted_iota(jnp.int32, (tq, tq), 1)
            s = jnp.where(cols <= rows, s, NEG_BIG)
        m_new = jnp.maximum(m, jnp.max(s, axis=-1, keepdims=True))
        alpha = jnp.exp(m - m_new)
        p = jnp.exp(s - m_new)
        l_new = alpha * l + jnp.sum(p, axis=-1, keepdims=True)
        acc_new = alpha * acc + jnp.dot(p.astype(BF16), vc, preferred_element_type=F32)
        return m_new, l_new, acc_new

    init = (jnp.full((tq, 1), NEG_BIG, F32), jnp.zeros((tq, 1), F32), jnp.zeros((tq, HEAD_DIM), F32))
    carry = lax.fori_loop(0, i, lambda j, c: chunk(j, c, False), init)
    _, l, acc = chunk(i, carry, True)
    o_ref[...] = (acc / l * gate_ref[...].astype(F32)).astype(o_ref.dtype)


def _fox_attention(qkvg, c):
    s, d4 = qkvg.shape
    d = d4 // 4
    heads = d // HEAD_DIM
    tq = _tile(s, 512)
    kern = functools.partial(_fox_attn_kernel, tq=tq)
    return pl.pallas_call(
        kern,
        out_shape=jax.ShapeDtypeStruct((s, d), BF16),
        grid=(heads, s // tq),
        in_specs=[pl.BlockSpec((tq, HEAD_DIM), lambda h, i: (i, h)),
                  pl.BlockSpec((s, HEAD_DIM), lambda h, i: (0, heads + h)),
                  pl.BlockSpec((s, HEAD_DIM), lambda h, i: (0, 2 * heads + h)),
                  pl.BlockSpec((None, 1, s), lambda h, i: (h, 0, 0)),
                  pl.BlockSpec((tq, HEAD_DIM), lambda h, i: (i, 3 * heads + h))],
        out_specs=pl.BlockSpec((tq, HEAD_DIM), lambda h, i: (i, h)),
        compiler_params=_params("parallel", "arbitrary"),
        name="fox_attention",
    )(qkvg, qkvg, qkvg, c.reshape(heads, 1, s), qkvg)


def _hgrn_level_widths(t):
    ws = []
    w = t // 2
    while w >= HGRN_DIAG:
        ws.append(w)
        w //= 2
    return ws


def _hgrn_sum_masks(t):
    r = jnp.arange(t)[:, None]
    c = jnp.arange(t)[None, :]
    blocks = [c <= r]
    for w in _hgrn_level_widths(t):
        ref = (r // (2 * w)) * (2 * w) + w - 1
        q_side = (r % (2 * w)) >= w
        blocks.append(jnp.where(q_side, (c > ref) & (c <= r), (c > r) & (c <= ref)))
    return jnp.concatenate(blocks, axis=0).astype(BF16)


def _hgrn_scan_kernel(q_ref, k_ref, g_ref, v_ref, gate_ref, gain_ref, masks_ref, ones_ref,
                      o_ref, state_ref, *, t):
    ci = pl.program_id(1)

    @pl.when(ci == 0)
    def _():
        state_ref[...] = jnp.zeros_like(state_ref)

    q = q_ref[...].astype(F32)
    k = k_ref[...].astype(F32)
    v_bf = v_ref[...]
    v = v_bf.astype(F32)
    g = g_ref[...]

    g_hi, g_mid, g_lo = _split_bf16(g, 3)
    masks = masks_ref[...]
    sums = (jnp.dot(masks, g_hi, preferred_element_type=F32)
            + jnp.dot(masks, g_mid, preferred_element_type=F32)
            + jnp.dot(masks, g_lo, preferred_element_type=F32))
    b = sums[0:t]
    b_last = b[t - 1:t]

    st = state_ref[...]
    o = lax.dot_general((q * jnp.exp(b)).astype(BF16), st.astype(BF16), _NT, preferred_element_type=F32)

    rows = lax.broadcasted_iota(jnp.int32, (t, 1), 0)
    scores = jnp.zeros((t, t), F32)
    for lvl, w in enumerate(_hgrn_level_widths(t)):
        fac = jnp.exp(sums[(lvl + 1) * t:(lvl + 2) * t])
        q_side = (rows % (2 * w)) >= w
        a = jnp.where(q_side, q * fac, 0.0).astype(BF16)
        bk = jnp.where(q_side, 0.0, k * fac).astype(BF16)
        sc = lax.dot_general(a, bk, _NT, preferred_element_type=F32)
        if 2 * w < t:
            ri = lax.broadcasted_iota(jnp.int32, (t, t), 0) // (2 * w)
            cj = lax.broadcasted_iota(jnp.int32, (t, t), 1) // (2 * w)
            sc = jnp.where(ri == cj, sc, 0.0)
        scores = scores + sc
    o = o + jnp.dot(scores.astype(BF16), v_bf, preferred_element_type=F32)

    sub = rows % HGRN_DIAG
    rel = jnp.zeros_like(g)
    ones = ones_ref[...]
    for dd in range(HGRN_DIAG):
        if dd == 0:
            x = q * k
            vd = v
        else:
            rel = rel + pltpu.roll(g, dd - 1, 0) if dd > 1 else rel + g
            valid = sub >= dd
            x = jnp.where(valid, q * pltpu.roll(k, dd, 0) * jnp.exp(jnp.where(valid, rel, NEG_BIG)), 0.0)
            vd = pltpu.roll(v, dd, 0)
        rowsum = jnp.dot(x.astype(BF16), ones, preferred_element_type=F32)
        o = o + rowsum * vd

    kh = (k * jnp.exp(b_last - b)).astype(BF16)
    state_ref[...] = st * jnp.exp(b_last) + lax.dot_general(v_bf, kh, _TN, preferred_element_type=F32)

    ms = jnp.mean(o * o, axis=-1, keepdims=True)
    y = o * lax.rsqrt(ms + NORM_EPS) * gain_ref[...]
    o_ref[...] = (y * gate_ref[...].astype(F32)).astype(o_ref.dtype)


def _hgrn_scan(qvg, g, k, out_gain):
    s, d = g.shape
    heads = d // HEAD_DIM
    t = _tile(s, HGRN_CHUNK)
    masks = _hgrn_sum_masks(t)
    nm = masks.shape[0]
    kern = functools.partial(_hgrn_scan_kernel, t=t)
    blk = lambda off: pl.BlockSpec((t, HEAD_DIM), lambda h, c: (c, off * heads + h))
    return pl.pallas_call(
        kern,
        out_shape=jax.ShapeDtypeStruct((s, d), BF16),
        grid=(heads, s // t),
        in_specs=[blk(0), blk(0), blk(0), blk(1), blk(2),
                  pl.BlockSpec((1, HEAD_DIM), lambda h, c: (0, 0)),
                  pl.BlockSpec((nm, t), lambda h, c: (0, 0)),
                  pl.BlockSpec((HEAD_DIM, HEAD_DIM), lambda h, c: (0, 0))],
        out_specs=blk(0),
        scratch_shapes=[pltpu.VMEM((HEAD_DIM, HEAD_DIM), F32)],
        compiler_params=_params("parallel", "arbitrary"),
        name="hgrn_scan",
    )(qvg, k, g, qvg, qvg, out_gain.reshape(1, HEAD_DIM).astype(F32), masks,
      jnp.ones((HEAD_DIM, HEAD_DIM), BF16))


def kernel(x, fox_w_in, fox_w_out, fox_q_gain, fox_k_gain, fox_fgate_bias, hgrn_w_in, hgrn_w_out,
           hgrn_out_gain, hgrn_lb_logits, mixer_norm_gain, mlp_norm_gain, mlp_w_up, mlp_w_down,
           final_norm_gain):
    batch, seq, d = x.shape
    depth = mixer_norm_gain.shape[0]
    outs = []
    for bi in range(batch):
        h = x[bi]
        for i in range(depth):
            j = i // 2
            n = _rmsnorm(h, mixer_norm_gain[i], BF16)
            if i % 2 == 0:
                w_in = fox_w_in[j]
                qkvg = _fox_in_proj(n, w_in[:, :4 * d].astype(BF16), fox_q_gain[j], fox_k_gain[j])
                c = _fox_forget_cumsum(n, w_in[:, 4 * d:].astype(BF16), fox_fgate_bias[j])
                mix_in = _fox_attention(qkvg, c)
                w_out = fox_w_out[j]
            else:
                w_in = hgrn_w_in[j].astype(BF16)
                qvg = _hgrn_in_proj_qvg(n, w_in)
                g, k = _hgrn_in_proj_forget(n, w_in, hgrn_lb_logits, i)
                mix_in = _hgrn_scan(qvg, g, k, hgrn_out_gain[j])
                w_out = hgrn_w_out[j]
            h = _proj_residual(mix_in, w_out.astype(BF16), h, name="mixer_out_proj")
            n = _rmsnorm(h, mlp_norm_gain[i], BF16)
            a = _mlp_up(n, mlp_w_up[i].astype(BF16))
            h = _proj_residual(a, mlp_w_down[i].astype(BF16), h, name="mlp_down")
        outs.append(_rmsnorm(h, final_norm_gain, x.dtype))
    return jnp.stack(outs, axis=0)
```

```python
import functools

import jax
import jax.numpy as jnp
from jax import lax
from jax.experimental import pallas as pl
from jax.experimental.pallas import tpu as pltpu

F32 = jnp.float32
BF16 = jnp.bfloat16

HEAD_DIM = 128
NORM_EPS = 1e-6
NEG_BIG = -1e30
VMEM_LIMIT_BYTES = 56 * 1024 * 1024

LOG2E = 1.4426950408889634
AUX_PIECES = 3
AUX_CONST_LANE = 96

HGRN_CHUNK = 256
HGRN_DIAG = 8

_NT = (((1,), (1,)), ((), ()))
_TN = (((0,), (0,)), ((), ()))


def _tile(n, pref):
    t = min(n, pref)
    while n % t:
        t //= 2
    return t


def _params(*sem):
    return pltpu.CompilerParams(dimension_semantics=sem, vmem_limit_bytes=VMEM_LIMIT_BYTES)


def _split_bf16(a, terms):
    parts = []
    r = a
    for _ in range(terms):
        p = r.astype(BF16)
        parts.append(p)
        r = r - p.astype(F32)
    return parts


def _sigmoid(x):
    return 1.0 / (1.0 + jnp.exp(-x))


def _log_sigmoid(x):
    return jnp.minimum(x, 0.0) - jnp.log(1.0 + jnp.exp(-jnp.abs(x)))


def _rmsnorm_kernel(x_ref, g_ref, o_ref):
    x = x_ref[...]
    ms = jnp.mean(x * x, axis=-1, keepdims=True)
    o_ref[...] = (x * lax.rsqrt(ms + NORM_EPS) * g_ref[...]).astype(o_ref.dtype)


def _rmsnorm(x, gain, out_dtype):
    s, d = x.shape
    tm = _tile(s, 256)
    return pl.pallas_call(
        _rmsnorm_kernel,
        out_shape=jax.ShapeDtypeStruct((s, d), out_dtype),
        grid=(s // tm,),
        in_specs=[pl.BlockSpec((tm, d), lambda i: (i, 0)),
                  pl.BlockSpec((1, d), lambda i: (0, 0))],
        out_specs=pl.BlockSpec((tm, d), lambda i: (i, 0)),
        compiler_params=_params("parallel"),
        name="rmsnorm",
    )(x, gain.reshape(1, d).astype(F32))


def _matmul_kernel(x_ref, w_ref, *refs, nk, n_aux, n_out, epilogue):
    aux = refs[:n_aux]
    outs = refs[n_aux:n_aux + n_out]
    part = jnp.dot(x_ref[...], w_ref[...], preferred_element_type=F32)
    if nk == 1:
        epilogue(part, aux, outs)
        return
    k = pl.program_id(2)
    acc_ref = outs[0]

    @pl.when(k == 0)
    def _():
        acc_ref[...] = part

    @pl.when(jnp.logical_and(k > 0, k < nk - 1))
    def _():
        acc_ref[...] += part

    @pl.when(k == nk - 1)
    def _():
        epilogue(acc_ref[...] + part, aux, outs)


def _matmul(x, w, *, tm, tn, tk, n_cols, w_col_block, epilogue, aux, aux_specs, out_shapes, name):
    m, kdim = x.shape
    nk = kdim // tk
    if nk > 1:
        assert len(out_shapes) == 1 and out_shapes[0].dtype == F32
    kern = functools.partial(_matmul_kernel, nk=nk, n_aux=len(aux), n_out=len(out_shapes),
                             epilogue=epilogue)
    out = pl.pallas_call(
        kern,
        out_shape=out_shapes,
        grid=(m // tm, n_cols // tn, nk),
        in_specs=[pl.BlockSpec((tm, tk), lambda i, j, k: (i, k)),
                  pl.BlockSpec((tk, tn), lambda i, j, k: (k, w_col_block(j)))] + list(aux_specs),
        out_specs=[pl.BlockSpec((tm, tn), lambda i, j, k: (i, j)) for _ in out_shapes],
        compiler_params=_params("parallel", "parallel", "arbitrary"),
        name=name,
    )(x, w, *aux)
    return out


def _head_rmsnorm_store(acc, gain, scale, o_ref):
    for c in range(acc.shape[1] // HEAD_DIM):
        sl = slice(c * HEAD_DIM, (c + 1) * HEAD_DIM)
        xc = acc[:, sl]
        ms = jnp.mean(xc * xc, axis=-1, keepdims=True)
        y = xc * lax.rsqrt(ms + NORM_EPS) * gain
        if scale != 1.0:
            y = y * scale
        o_ref[:, sl] = y.astype(o_ref.dtype)


def _fox_in_proj(n, w_qkvg, q_gain, k_gain):
    s, d = n.shape
    tm, tn = _tile(s, 1024), _tile(d, 512)
    nq = d // tn
    scale = HEAD_DIM ** -0.5 * LOG2E

    def epilogue(acc, aux, outs):
        qg_ref, kg_ref = aux
        (o_ref,) = outs
        j = pl.program_id(1)

        @pl.when(j < nq)
        def _():
            _head_rmsnorm_store(acc, qg_ref[...], scale, o_ref)

        @pl.when(jnp.logical_and(j >= nq, j < 2 * nq))
        def _():
            _head_rmsnorm_store(acc, kg_ref[...], 1.0, o_ref)

        @pl.when(jnp.logical_and(j >= 2 * nq, j < 3 * nq))
        def _():
            o_ref[...] = acc.astype(o_ref.dtype)

        @pl.when(j >= 3 * nq)
        def _():
            o_ref[...] = _sigmoid(acc).astype(o_ref.dtype)

    gspec = pl.BlockSpec((1, HEAD_DIM), lambda i, j, k: (0, 0))
    (out,) = _matmul(n, w_qkvg, tm=tm, tn=tn, tk=d, n_cols=4 * d, w_col_block=lambda j: j,
                     epilogue=epilogue,
                     aux=(q_gain.reshape(1, HEAD_DIM).astype(F32), k_gain.reshape(1, HEAD_DIM).astype(F32)),
                     aux_specs=(gspec, gspec),
                     out_shapes=[jax.ShapeDtypeStruct((s, 4 * d), BF16)], name="fox_in_proj")
    return out


def _hgrn_in_proj_qvg(n, w_in):
    s, d = n.shape
    tm, tn = _tile(s, 1024), _tile(d, 512)
    nq = d // tn

    def epilogue(acc, aux, outs):
        (o_ref,) = outs
        j = pl.program_id(1)
        is_v = jnp.logical_and(j >= nq, j < 2 * nq)

        @pl.when(is_v)
        def _():
            o_ref[...] = acc.astype(o_ref.dtype)

        @pl.when(jnp.logical_not(is_v))
        def _():
            o_ref[...] = (acc * _sigmoid(acc)).astype(o_ref.dtype)

    (out,) = _matmul(n, w_in, tm=tm, tn=tn, tk=d, n_cols=3 * d,
                     w_col_block=lambda j: jnp.where(j < nq, j, j + nq),
                     epilogue=epilogue, aux=(), aux_specs=(),
                     out_shapes=[jax.ShapeDtypeStruct((s, 3 * d), BF16)], name="hgrn_in_proj_qvg")
    return out


def _hgrn_in_proj_forget(n, w_in, lb_logits, layer):
    s, d = n.shape
    depth = lb_logits.shape[0]
    tm, tn = _tile(s, 1024), _tile(d, 512)
    nq = d // tn

    def epilogue(acc, aux, outs):
        (lb_ref,) = aux
        g_ref, k_ref = outs
        z = lb_ref[...]
        e = jnp.exp(z - jnp.max(z, axis=0, keepdims=True))
        p = e / jnp.sum(e, axis=0, keepdims=True)
        lb = jnp.zeros((1, tn), F32)
        for r in range(1, layer + 1):
            lb = lb + p[r:r + 1, :]
        f = lb + (1.0 - lb) * _sigmoid(acc)
        g_ref[...] = jnp.log(f)
        k_ref[...] = (1.0 - f).astype(k_ref.dtype)

    g, k = _matmul(n, w_in, tm=tm, tn=tn, tk=d, n_cols=d, w_col_block=lambda j: j + nq,
                   epilogue=epilogue, aux=(lb_logits.astype(F32),),
                   aux_specs=(pl.BlockSpec((depth, tn), lambda i, j, k: (0, j)),),
                   out_shapes=[jax.ShapeDtypeStruct((s, d), F32), jax.ShapeDtypeStruct((s, d), BF16)],
                   name="hgrn_in_proj_forget")
    return g, k


def _proj_residual(x, w, h, *, name):
    s, kdim = x.shape
    d = w.shape[1]
    tm, tn = _tile(s, 1024), _tile(d, 1024)
    tk = _tile(kdim, 4096 if kdim <= 4096 else 2048)
    if kdim // tk == 1:
        tn = _tile(d, 512)

    def epilogue(acc, aux, outs):
        (h_ref,) = aux
        (o_ref,) = outs
        o_ref[...] = h_ref[...] + acc

    (out,) = _matmul(x, w, tm=tm, tn=tn, tk=tk, n_cols=d, w_col_block=lambda j: j,
                     epilogue=epilogue, aux=(h,),
                     aux_specs=(pl.BlockSpec((tm, tn), lambda i, j, k: (i, j)),),
                     out_shapes=[jax.ShapeDtypeStruct((s, d), F32)], name=name)
    return out


def _mlp_up(n, w_up):
    s, d = n.shape
    f = w_up.shape[1]
    tm, tn = _tile(s, 1024), _tile(f, 512)

    def epilogue(acc, aux, outs):
        (o_ref,) = outs
        r = jnp.maximum(acc, 0.0)
        o_ref[...] = (r * r).astype(o_ref.dtype)

    (out,) = _matmul(n, w_up, tm=tm, tn=tn, tk=d, n_cols=f, w_col_block=lambda j: j,
                     epilogue=epilogue, aux=(), aux_specs=(),
                     out_shapes=[jax.ShapeDtypeStruct((s, f), BF16)], name="mlp_up")
    return out


def _fox_forget_kernel(x_ref, wf_ref, bias_ref, tri_ref, perm_ref, c_ref, kaux_ref, carry_ref):
    i = pl.program_id(0)

    @pl.when(i == 0)
    def _():
        carry_ref[...] = jnp.zeros_like(carry_ref)

    fz = jnp.dot(x_ref[...], wf_ref[...], preferred_element_type=F32)
    lf = _log_sigmoid(fz + bias_ref[...]) * LOG2E
    tri = tri_ref[...]
    cs = jnp.zeros_like(lf)
    for part in _split_bf16(lf, 3):
        cs = cs + jnp.dot(tri, part, preferred_element_type=F32)
    c = cs + carry_ref[...]
    c_ref[...] = c
    carry_ref[...] = c[c.shape[0] - 1:, :]

    pieces = jnp.concatenate(_split_bf16(c, AUX_PIECES), axis=1)
    lane = lax.broadcasted_iota(jnp.int32, (1, HEAD_DIM), 1)
    const = jnp.where(jnp.logical_and(lane >= AUX_CONST_LANE, lane < AUX_CONST_LANE + AUX_PIECES), 1.0, 0.0)
    kaux = jnp.dot(pieces, perm_ref[...], preferred_element_type=F32) + const
    kaux_ref[...] = kaux.astype(kaux_ref.dtype)


def _fox_forget_cumsum(n, w_f, bias):
    s, d = n.shape
    heads = w_f.shape[1]
    assert AUX_PIECES * heads <= AUX_CONST_LANE
    tm = _tile(s, 512)
    r = jnp.arange(tm)
    tri = (r[None, :] <= r[:, None]).astype(BF16)
    wf = jnp.zeros((d, HEAD_DIM), BF16).at[:, :heads].set(w_f)
    b = jnp.zeros((1, HEAD_DIM), F32).at[0, :heads].set(bias.astype(F32))
    hh = jnp.arange(heads)
    perm = jnp.zeros((AUX_PIECES * HEAD_DIM, HEAD_DIM), BF16)
    for p in range(AUX_PIECES):
        perm = perm.at[p * HEAD_DIM + hh, AUX_PIECES * hh + p].set(-1.0)
    return pl.pallas_call(
        _fox_forget_kernel,
        out_shape=[jax.ShapeDtypeStruct((s, HEAD_DIM), F32), jax.ShapeDtypeStruct((s, HEAD_DIM), BF16)],
        grid=(s // tm,),
        in_specs=[pl.BlockSpec((tm, d), lambda i: (i, 0)),
                  pl.BlockSpec((d, HEAD_DIM), lambda i: (0, 0)),
                  pl.BlockSpec((1, HEAD_DIM), lambda i: (0, 0)),
                  pl.BlockSpec((tm, tm), lambda i: (0, 0)),
                  pl.BlockSpec((AUX_PIECES * HEAD_DIM, HEAD_DIM), lambda i: (0, 0))],
        out_specs=[pl.BlockSpec((tm, HEAD_DIM), lambda i: (i, 0)),
                   pl.BlockSpec((tm, HEAD_DIM), lambda i: (i, 0))],
        scratch_shapes=[pltpu.VMEM((1, HEAD_DIM), F32)],
        compiler_params=_params("arbitrary"),
        name="fox_forget_cumsum",
    )(n, wf, b, tri, perm)


def _fox_attn_kernel(q_ref, k_ref, v_ref, kaux_ref, c_ref, gate_ref, o_ref,
                     st_a, st_b, p_a, p_b, acc_ref, *, tq):
    h = pl.program_id(0)
    i = pl.program_id(1)
    lane = lax.broadcasted_iota(jnp.int32, (1, HEAD_DIM), 1)
    c0 = jnp.sum(jnp.where(lane == h, c_ref[0:1, :], 0.0), axis=1, keepdims=True)
    qaux = jnp.where(jnp.logical_and(lane >= AUX_PIECES * h, lane < AUX_PIECES * (h + 1)), 1.0, 0.0)
    for p, piece in enumerate(_split_bf16(c0, AUX_PIECES)):
        qaux = jnp.where(lane == AUX_CONST_LANE + p, piece.astype(F32), qaux)
    q_aug = jnp.concatenate(
        [q_ref[...], jnp.broadcast_to(qaux.astype(BF16), (tq, HEAD_DIM))], axis=1)

    tk = tq // 2

    def logits(c):
        row0 = pl.multiple_of(c * tk, tk)
        k_aug = jnp.concatenate([k_ref[pl.ds(row0, tk), :], kaux_ref[pl.ds(row0, tk), :]], axis=1)
        return lax.dot_general(k_aug, q_aug, _NT, preferred_element_type=F32)

    def weighted_values(c, p):
        row0 = pl.multiple_of(c * tk, tk)
        return lax.dot_general(v_ref[pl.ds(row0, tk), :], p, _TN, preferred_element_type=F32)

    def step(c, m, l, st_cur, p_prev, p_cur, key0=None):
        acc = acc_ref[...] + weighted_values(jnp.maximum(c - 1, 0), p_prev[...])
        st = st_cur[...]
        if key0 is None:
            st_cur[...] = logits(c + 2)
        else:
            keys = lax.broadcasted_iota(jnp.int32, (tk, tq), 0) + key0
            queries = lax.broadcasted_iota(jnp.int32, (tk, tq), 1)
            st = jnp.where(keys <= queries, st, NEG_BIG)
        m_new = jnp.maximum(m, jnp.max(st, axis=0, keepdims=True))
        alpha = jnp.exp2(m - m_new)
        p = jnp.exp2(st - m_new)
        l_new = alpha * l + jnp.sum(p, axis=0, keepdims=True)
        p_cur[...] = p.astype(BF16)
        acc_ref[...] = alpha * acc
        return m_new, l_new

    st_a[...] = logits(0)
    st_b[...] = logits(1)
    p_b[...] = jnp.zeros_like(p_b)
    acc_ref[...] = jnp.zeros_like(acc_ref)

    def pair(jj, carry):
        m, l = carry
        m, l = step(2 * jj, m, l, st_a, p_b, p_a)
        return step(2 * jj + 1, m, l, st_b, p_a, p_b)

    m, l = lax.fori_loop(0, i, pair, (jnp.full((1, tq), NEG_BIG, F32), jnp.zeros((1, tq), F32)))
    m, l = step(2 * i, m, l, st_a, p_b, p_a, key0=0)
    m, l = step(2 * i + 1, m, l, st_b, p_a, p_b, key0=tk)
    acc = acc_ref[...] + weighted_values(2 * i + 1, p_b[...])
    o_ref[...] = ((acc / l).T * gate_ref[...].astype(F32)).astype(o_ref.dtype)


def _fox_attention(qkvg, c, kaux):
    s, d4 = qkvg.shape
    d = d4 // 4
    heads = d // HEAD_DIM
    tq = _tile(s, 1024)
    kern = functools.partial(_fox_attn_kernel, tq=tq)
    return pl.pallas_call(
        kern,
        out_shape=jax.ShapeDtypeStruct((s, d), BF16),
        grid=(heads, s // tq),
        in_specs=[pl.BlockSpec((tq, HEAD_DIM), lambda h, i: (i, h)),
                  pl.BlockSpec((s, HEAD_DIM), lambda h, i: (0, heads + h)),
                  pl.BlockSpec((s, HEAD_DIM), lambda h, i: (0, 2 * heads + h)),
                  pl.BlockSpec((s, HEAD_DIM), lambda h, i: (0, 0)),
                  pl.BlockSpec((8, HEAD_DIM), lambda h, i: (i * (tq // 8), 0)),
                  pl.BlockSpec((tq, HEAD_DIM), lambda h, i: (i, 3 * heads + h))],
        out_specs=pl.BlockSpec((tq, HEAD_DIM), lambda h, i: (i, h)),
        scratch_shapes=[pltpu.VMEM((tq // 2, tq), F32), pltpu.VMEM((tq // 2, tq), F32),
                        pltpu.VMEM((tq // 2, tq), BF16), pltpu.VMEM((tq // 2, tq), BF16),
                        pltpu.VMEM((HEAD_DIM, tq), F32)],
        compiler_params=_params("parallel", "arbitrary"),
        name="fox_attention",
    )(qkvg, qkvg, qkvg, kaux, c, qkvg)


def _hgrn_level_widths(t):
    ws = []
    w = t // 2
    while w >= HGRN_DIAG:
        ws.append(w)
        w //= 2
    return ws


def _hgrn_sum_masks(t):
    r = jnp.arange(t)[:, None]
    c = jnp.arange(t)[None, :]
    blocks = [c <= r]
    for w in _hgrn_level_widths(t):
        ref = (r // (2 * w)) * (2 * w) + w - 1
        q_side = (r % (2 * w)) >= w
        blocks.append(jnp.where(q_side, (c > ref) & (c <= r), (c > r) & (c <= ref)))
    return jnp.concatenate(blocks, axis=0).astype(BF16)


def _hgrn_scan_kernel(q_ref, k_ref, g_ref, v_ref, gate_ref, gain_ref, masks_ref, ones_ref,
                      o_ref, state_ref, *, t):
    ci = pl.program_id(1)

    @pl.when(ci == 0)
    def _():
        state_ref[...] = jnp.zeros_like(state_ref)

    q = q_ref[...].astype(F32)
    k = k_ref[...].astype(F32)
    v_bf = v_ref[...]
    v = v_bf.astype(F32)
    g = g_ref[...]

    pieces = jnp.concatenate(_split_bf16(g, 2), axis=1)
    sums2 = jnp.dot(masks_ref[...], pieces, preferred_element_type=F32)
    sums = sums2[:, :HEAD_DIM] + sums2[:, HEAD_DIM:]
    b = sums[0:t]
    b_last = b[t - 1:t]

    st = state_ref[...]
    o = lax.dot_general((q * jnp.exp(b)).astype(BF16), st.astype(BF16), _NT, preferred_element_type=F32)

    rows = lax.broadcasted_iota(jnp.int32, (t, 1), 0)
    scores = jnp.zeros((t, t), F32)
    for lvl, w in enumerate(_hgrn_level_widths(t)):
        fac = jnp.exp(sums[(lvl + 1) * t:(lvl + 2) * t])
        q_side = (rows % (2 * w)) >= w
        a = jnp.where(q_side, q * fac, 0.0).astype(BF16)
        bk = jnp.where(q_side, 0.0, k * fac).astype(BF16)
        sc = lax.dot_general(a, bk, _NT, preferred_element_type=F32)
        if 2 * w < t:
            ri = lax.broadcasted_iota(jnp.int32, (t, t), 0) // (2 * w)
            cj = lax.broadcasted_iota(jnp.int32, (t, t), 1) // (2 * w)
            sc = jnp.where(ri == cj, sc, 0.0)
        scores = scores + sc
    o = o + jnp.dot(scores.astype(BF16), v_bf, preferred_element_type=F32)

    def group_roll(a, dd):
        return pltpu.roll(a.reshape(t // HGRN_DIAG, HGRN_DIAG, HEAD_DIM), dd, 1).reshape(t, HEAD_DIM)

    sub = rows % HGRN_DIAG
    rel = jnp.zeros_like(g)
    xs, vds = [q * k], [v]
    for dd in range(1, HGRN_DIAG):
        rel = rel + (group_roll(g, dd - 1) if dd > 1 else g)
        xs.append(q * group_roll(k, dd) * jnp.exp(jnp.where(sub >= dd, rel, NEG_BIG)))
        vds.append(group_roll(v, dd))
    ones2 = ones_ref[...]
    for dd in range(0, HGRN_DIAG, 2):
        pair = jnp.concatenate([xs[dd], xs[dd + 1]], axis=1).astype(BF16)
        rowsums = jnp.dot(pair, ones2, preferred_element_type=F32)
        o = o + rowsums[:, :HEAD_DIM] * vds[dd] + rowsums[:, HEAD_DIM:] * vds[dd + 1]

    kh = (k * jnp.exp(b_last - b)).astype(BF16)
    state_ref[...] = st * jnp.exp(b_last) + lax.dot_general(v_bf, kh, _TN, preferred_element_type=F32)

    ms = jnp.mean(o * o, axis=-1, keepdims=True)
    y = o * lax.rsqrt(ms + NORM_EPS) * gain_ref[...]
    o_ref[...] = (y * gate_ref[...].astype(F32)).astype(o_ref.dtype)


def _hgrn_scan(qvg, g, k, out_gain):
    s, d = g.shape
    heads = d // HEAD_DIM
    t = _tile(s, HGRN_CHUNK)
    masks = _hgrn_sum_masks(t)
    nm = masks.shape[0]
    kern = functools.partial(_hgrn_scan_kernel, t=t)
    blk = lambda off: pl.BlockSpec((t, HEAD_DIM), lambda h, c: (c, off * heads + h))
    return pl.pallas_call(
        kern,
        out_shape=jax.ShapeDtypeStruct((s, d), BF16),
        grid=(heads, s // t),
        in_specs=[blk(0), blk(0), blk(0), blk(1), blk(2),
                  pl.BlockSpec((1, HEAD_DIM), lambda h, c: (0, 0)),
                  pl.BlockSpec((nm, t), lambda h, c: (0, 0)),
                  pl.BlockSpec((2 * HEAD_DIM, 2 * HEAD_DIM), lambda h, c: (0, 0))],
        out_specs=blk(0),
        scratch_shapes=[pltpu.VMEM((HEAD_DIM, HEAD_DIM), F32)],
        compiler_params=_params("parallel", "arbitrary"),
        name="hgrn_scan",
    )(qvg, k, g, qvg, qvg, out_gain.reshape(1, HEAD_DIM).astype(F32), masks,
      jnp.kron(jnp.eye(2, dtype=BF16), jnp.ones((HEAD_DIM, HEAD_DIM), BF16)))


def kernel(x, fox_w_in, fox_w_out, fox_q_gain, fox_k_gain, fox_fgate_bias, hgrn_w_in, hgrn_w_out,
           hgrn_out_gain, hgrn_lb_logits, mixer_norm_gain, mlp_norm_gain, mlp_w_up, mlp_w_down,
           final_norm_gain):
    batch, seq, d = x.shape
    depth = mixer_norm_gain.shape[0]
    outs = []
    for bi in range(batch):
        h = x[bi]
        for i in range(depth):
            j = i // 2
            n = _rmsnorm(h, mixer_norm_gain[i], BF16)
            if i % 2 == 0:
                w_in = fox_w_in[j]
                qkvg = _fox_in_proj(n, w_in[:, :4 * d].astype(BF16), fox_q_gain[j], fox_k_gain[j])
                c, kaux = _fox_forget_cumsum(n, w_in[:, 4 * d:].astype(BF16), fox_fgate_bias[j])
                mix_in = _fox_attention(qkvg, c, kaux)
                w_out = fox_w_out[j]
            else:
                w_in = hgrn_w_in[j].astype(BF16)
                qvg = _hgrn_in_proj_qvg(n, w_in)
                g, k = _hgrn_in_proj_forget(n, w_in, hgrn_lb_logits, i)
                mix_in = _hgrn_scan(qvg, g, k, hgrn_out_gain[j])
                w_out = hgrn_w_out[j]
            h = _proj_residual(mix_in, w_out.astype(BF16), h, name="mixer_out_proj")
            n = _rmsnorm(h, mlp_norm_gain[i], BF16)
            a = _mlp_up(n, mlp_w_up[i].astype(BF16))
            h = _proj_residual(a, mlp_w_down[i].astype(BF16), h, name="mlp_down")
        outs.append(_rmsnorm(h, final_norm_gain, x.dtype))
    return jnp.stack(outs, axis=0)
```

```python
import functools

import jax
import jax.numpy as jnp
from jax import lax
from jax.experimental import pallas as pl
from jax.experimental.pallas import tpu as pltpu

F32 = jnp.float32
BF16 = jnp.bfloat16

HEAD_DIM = 128
NORM_EPS = 1e-6
NEG_BIG = -1e30
VMEM_LIMIT_BYTES = 56 * 1024 * 1024

LOG2E = 1.4426950408889634
AUX_PIECES = 3
AUX_CONST_LANE = 96

ATTN_HEADS_PER_STEP = 1

HGRN_CHUNK = 256
HGRN_HEADS_PER_STEP = 2
HGRN_DIAG = 8

_NT = (((1,), (1,)), ((), ()))
_TN = (((0,), (0,)), ((), ()))


def _tile(n, pref):
    t = min(n, pref)
    while n % t:
        t //= 2
    return t


def _params(*sem):
    return pltpu.CompilerParams(dimension_semantics=sem, vmem_limit_bytes=VMEM_LIMIT_BYTES)


def _split_bf16(a, terms):
    parts = []
    r = a
    for _ in range(terms):
        p = r.astype(BF16)
        parts.append(p)
        r = r - p.astype(F32)
    return parts


def _sigmoid(x):
    return 1.0 / (1.0 + jnp.exp(-x))


def _log_sigmoid(x):
    return jnp.minimum(x, 0.0) - jnp.log(1.0 + jnp.exp(-jnp.abs(x)))


def _rmsnorm_kernel(x_ref, g_ref, o_ref):
    x = x_ref[...]
    ms = jnp.mean(x * x, axis=-1, keepdims=True)
    o_ref[...] = (x * lax.rsqrt(ms + NORM_EPS) * g_ref[...]).astype(o_ref.dtype)


def _rmsnorm(x, gain, out_dtype):
    s, d = x.shape
    tm = _tile(s, 256)
    return pl.pallas_call(
        _rmsnorm_kernel,
        out_shape=jax.ShapeDtypeStruct((s, d), out_dtype),
        grid=(s // tm,),
        in_specs=[pl.BlockSpec((tm, d), lambda i: (i, 0)),
                  pl.BlockSpec((1, d), lambda i: (0, 0))],
        out_specs=pl.BlockSpec((tm, d), lambda i: (i, 0)),
        compiler_params=_params("parallel"),
        name="rmsnorm",
    )(x, gain.reshape(1, d).astype(F32))


def _matmul_kernel(x_ref, w_ref, *refs, nk, n_aux, epilogue):
    aux, outs = refs[:n_aux], refs[n_aux:]
    if nk == 1:
        epilogue(jnp.dot(x_ref[...], w_ref[...], preferred_element_type=F32), aux, outs)
        return
    (h_ref,), (o_ref,) = aux, outs

    @pl.when(pl.program_id(2) == 0)
    def _():
        o_ref[...] = h_ref[...]

    o_ref[...] += jnp.dot(x_ref[...], w_ref[...], preferred_element_type=F32)


def _matmul(x, w, layer, *, tm, tn, tk, n_cols, w_col_block, epilogue, aux, aux_specs, out_shapes, name,
            out_specs=None):
    m, kdim = x.shape
    nk = kdim // tk
    if nk > 1:
        assert epilogue is None and len(aux) == 1 and len(out_shapes) == 1 and out_shapes[0].dtype == F32
    kern = functools.partial(_matmul_kernel, nk=nk, n_aux=len(aux), epilogue=epilogue)
    return pl.pallas_call(
        kern,
        out_shape=out_shapes,
        grid=(m // tm, n_cols // tn, nk),
        in_specs=[pl.BlockSpec((tm, tk), lambda i, j, k: (i, k)),
                  pl.BlockSpec((None, tk, tn), lambda i, j, k: (layer, k, w_col_block(j)))] + list(aux_specs),
        out_specs=out_specs or [pl.BlockSpec((tm, tn), lambda i, j, k: (i, j)) for _ in out_shapes],
        compiler_params=_params("parallel", "parallel", "arbitrary"),
        name=name,
    )(x, w, *aux)


def _silu(x):
    return x * _sigmoid(x)


def _in_proj(n, w, layer, col0, name, *, act=None, head_gain=None, head_major=False):
    s, d = n.shape
    tm, tn = _tile(s, 1024), _tile(d, 512)
    blk0 = col0 // tn
    hpt = tn // HEAD_DIM

    def epilogue(acc, aux, outs):
        (o_ref,) = outs
        for c in range(hpt):
            xc = acc[:, c * HEAD_DIM:(c + 1) * HEAD_DIM]
            if head_gain is not None:
                ms = jnp.mean(xc * xc, axis=-1, keepdims=True)
                xc = xc * lax.rsqrt(ms + NORM_EPS) * aux[0][...]
            elif act is not None:
                xc = act(xc)
            if head_major:
                o_ref[c] = xc.astype(o_ref.dtype)
            else:
                o_ref[:, c * HEAD_DIM:(c + 1) * HEAD_DIM] = xc.astype(o_ref.dtype)

    aux, aux_specs = (), ()
    if head_gain is not None:
        aux = (head_gain.astype(F32).reshape(1, HEAD_DIM),)
        aux_specs = (pl.BlockSpec((1, HEAD_DIM), lambda i, j, k: (0, 0)),)
    if head_major:
        out_shape = jax.ShapeDtypeStruct((d // HEAD_DIM, s, HEAD_DIM), BF16)
        out_specs = [pl.BlockSpec((hpt, tm, HEAD_DIM), lambda i, j, k: (j, i, 0))]
    else:
        out_shape, out_specs = jax.ShapeDtypeStruct((s, d), BF16), None
    (out,) = _matmul(n, w, layer, tm=tm, tn=tn, tk=d, n_cols=d, w_col_block=lambda j: j + blk0,
                     epilogue=epilogue, aux=aux, aux_specs=aux_specs,
                     out_shapes=[out_shape], out_specs=out_specs, name=name)
    return out


def _hgrn_in_proj_forget(n, w_in, w_layer, lb_logits, layer):
    s, d = n.shape
    depth = lb_logits.shape[0]
    tm, tn = _tile(s, 1024), _tile(d, 512)
    nq = d // tn

    def epilogue(acc, aux, outs):
        (lb_ref,) = aux
        g_ref, k_ref = outs
        z = lb_ref[...]
        e = jnp.exp(z - jnp.max(z, axis=0, keepdims=True))
        p = e / jnp.sum(e, axis=0, keepdims=True)
        lb = jnp.zeros((1, tn), F32)
        for r in range(1, layer + 1):
            lb = lb + p[r:r + 1, :]
        f = lb + (1.0 - lb) * _sigmoid(acc)
        g_ref[...] = jnp.log(f)
        k_ref[...] = (1.0 - f).astype(k_ref.dtype)

    g, k = _matmul(n, w_in, w_layer, tm=tm, tn=tn, tk=d, n_cols=d, w_col_block=lambda j: j + nq,
                   epilogue=epilogue, aux=(lb_logits.astype(F32),),
                   aux_specs=(pl.BlockSpec((depth, tn), lambda i, j, k: (0, j)),),
                   out_shapes=[jax.ShapeDtypeStruct((s, d), F32), jax.ShapeDtypeStruct((s, d), BF16)],
                   name="hgrn_in_proj_forget")
    return g, k


def _proj_residual(x, w, layer, h, *, name):
    s, kdim = x.shape
    d = w.shape[2]
    tm = _tile(s, 1024)
    if kdim <= 4096:
        tn, tk = _tile(d, 512), kdim

        def epilogue(acc, aux, outs):
            (h_ref,) = aux
            (o_ref,) = outs
            o_ref[...] = h_ref[...] + acc
    else:
        tn, tk, epilogue = _tile(d, 2048), 1024, None

    (out,) = _matmul(x, w, layer, tm=tm, tn=tn, tk=tk, n_cols=d, w_col_block=lambda j: j,
                     epilogue=epilogue, aux=(h,),
                     aux_specs=(pl.BlockSpec((tm, tn), lambda i, j, k: (i, j)),),
                     out_shapes=[jax.ShapeDtypeStruct((s, d), F32)], name=name)
    return out


def _mlp_up(n, w_up, layer):
    s, d = n.shape
    f = w_up.shape[2]
    tm, tn = _tile(s, 1024), _tile(f, 512)

    def epilogue(acc, aux, outs):
        (o_ref,) = outs
        r = jnp.maximum(acc, 0.0)
        o_ref[...] = (r * r).astype(o_ref.dtype)

    (out,) = _matmul(n, w_up, layer, tm=tm, tn=tn, tk=d, n_cols=f, w_col_block=lambda j: j,
                     epilogue=epilogue, aux=(), aux_specs=(),
                     out_shapes=[jax.ShapeDtypeStruct((s, f), BF16)], name="mlp_up")
    return out


def _fox_forget_kernel(x_ref, wf_ref, bias_ref, tri_ref, perm_ref, c_ref, kaux_ref, carry_ref):
    i = pl.program_id(0)

    @pl.when(i == 0)
    def _():
        carry_ref[...] = jnp.zeros_like(carry_ref)

    fz = jnp.dot(x_ref[...], wf_ref[...], preferred_element_type=F32)
    lf = _log_sigmoid(fz + bias_ref[...]) * LOG2E
    tri = tri_ref[...]
    cs = jnp.zeros_like(lf)
    for part in _split_bf16(lf, 3):
        cs = cs + jnp.dot(tri, part, preferred_element_type=F32)
    c = cs + carry_ref[...]
    c_ref[...] = c
    carry_ref[...] = c[c.shape[0] - 1:, :]

    pieces = jnp.concatenate(_split_bf16(c, AUX_PIECES), axis=1)
    lane = lax.broadcasted_iota(jnp.int32, (1, HEAD_DIM), 1)
    const = jnp.where(jnp.logical_and(lane >= AUX_CONST_LANE, lane < AUX_CONST_LANE + AUX_PIECES), 1.0, 0.0)
    kaux = jnp.dot(pieces, perm_ref[...], preferred_element_type=F32) + const
    kaux_ref[...] = kaux.astype(kaux_ref.dtype)


def _fox_forget_cumsum(n, w_f, bias):
    s, d = n.shape
    heads = w_f.shape[1]
    assert AUX_PIECES * heads <= AUX_CONST_LANE
    tm = _tile(s, 512)
    r = jnp.arange(tm)
    tri = (r[None, :] <= r[:, None]).astype(BF16)
    wf = jnp.zeros((d, HEAD_DIM), BF16).at[:, :heads].set(w_f)
    b = jnp.zeros((1, HEAD_DIM), F32).at[0, :heads].set(bias.astype(F32))
    hh = jnp.arange(heads)
    perm = jnp.zeros((AUX_PIECES * HEAD_DIM, HEAD_DIM), BF16)
    for p in range(AUX_PIECES):
        perm = perm.at[p * HEAD_DIM + hh, AUX_PIECES * hh + p].set(-1.0)
    return pl.pallas_call(
        _fox_forget_kernel,
        out_shape=[jax.ShapeDtypeStruct((s, HEAD_DIM), F32), jax.ShapeDtypeStruct((s, HEAD_DIM), BF16)],
        grid=(s // tm,),
        in_specs=[pl.BlockSpec((tm, d), lambda i: (i, 0)),
                  pl.BlockSpec((d, HEAD_DIM), lambda i: (0, 0)),
                  pl.BlockSpec((1, HEAD_DIM), lambda i: (0, 0)),
                  pl.BlockSpec((tm, tm), lambda i: (0, 0)),
                  pl.BlockSpec((AUX_PIECES * HEAD_DIM, HEAD_DIM), lambda i: (0, 0))],
        out_specs=[pl.BlockSpec((tm, HEAD_DIM), lambda i: (i, 0)),
                   pl.BlockSpec((tm, HEAD_DIM), lambda i: (i, 0))],
        scratch_shapes=[pltpu.VMEM((1, HEAD_DIM), F32)],
        compiler_params=_params("arbitrary"),
        name="fox_forget_cumsum",
    )(n, wf, b, tri, perm)


def _fox_attn_kernel(q_ref, k_ref, v_ref, kaux_ref, c_ref, gate_ref, o_ref,
                     st_a, st_b, cmax_a, cmax_b, p_a, p_b, acc_ref, *, tq, heads_per_step):
    i = pl.program_id(1)
    tk = tq // 2
    lane = lax.broadcasted_iota(jnp.int32, (1, HEAD_DIM), 1)
    hs = range(heads_per_step)

    q_aug = []
    for hh in hs:
        h = pl.program_id(0) * heads_per_step + hh
        c0 = jnp.sum(jnp.where(lane == h, c_ref[0:1, :], 0.0), axis=1, keepdims=True)
        qaux = jnp.where(jnp.logical_and(lane >= AUX_PIECES * h, lane < AUX_PIECES * (h + 1)), 1.0, 0.0)
        for p, piece in enumerate(_split_bf16(c0, AUX_PIECES)):
            qaux = jnp.where(lane == AUX_CONST_LANE + p, piece.astype(F32), qaux)
        q_aug.append(jnp.concatenate(
            [q_ref[:, hh * HEAD_DIM:(hh + 1) * HEAD_DIM],
             jnp.broadcast_to(qaux.astype(BF16), (tq, HEAD_DIM))], axis=1))

    def logits(hh, c):
        row0 = pl.multiple_of(c * tk, tk)
        k_aug = jnp.concatenate([k_ref[hh, pl.ds(row0, tk), :], kaux_ref[pl.ds(row0, tk), :]], axis=1)
        return lax.dot_general(k_aug, q_aug[hh], _NT, preferred_element_type=F32)

    def weighted_values(hh, c, p):
        row0 = pl.multiple_of(c * tk, tk)
        return lax.dot_general(v_ref[hh, pl.ds(row0, tk), :], p, _TN, preferred_element_type=F32)

    def fill(hh, c, st_buf, cmax_buf):
        st = logits(hh, c)
        st_buf[hh] = st
        cmax_buf[hh] = jnp.max(st, axis=0, keepdims=True)

    def step(hh, c, m, l, st_cur, cmax_cur, p_prev, p_cur, key0=None):
        acc = acc_ref[hh] + weighted_values(hh, jnp.maximum(c - 1, 0), p_prev[hh])
        st = st_cur[hh]
        if key0 is None:
            cmax = cmax_cur[hh]
            fill(hh, c + 2, st_cur, cmax_cur)
        else:
            keys = lax.broadcasted_iota(jnp.int32, (tk, tq), 0) + key0
            queries = lax.broadcasted_iota(jnp.int32, (tk, tq), 1)
            st = jnp.where(keys <= queries, st, NEG_BIG)
            cmax = jnp.max(st, axis=0, keepdims=True)
        m_new = jnp.maximum(m, cmax)
        alpha = jnp.exp2(m - m_new)
        p = jnp.exp2(st - m_new)
        l_new = alpha * l + jnp.sum(p, axis=0, keepdims=True)
        p_cur[hh] = p.astype(BF16)
        acc_ref[hh] = alpha * acc
        return m_new, l_new

    def steps(c, ml, st_cur, cmax_cur, p_prev, p_cur, key0=None):
        return tuple(step(hh, c, *ml[hh], st_cur, cmax_cur, p_prev, p_cur, key0) for hh in hs)

    for hh in hs:
        fill(hh, 0, st_a, cmax_a)
        fill(hh, 1, st_b, cmax_b)
    p_b[...] = jnp.zeros_like(p_b)
    acc_ref[...] = jnp.zeros_like(acc_ref)

    def pair(jj, ml):
        ml = steps(2 * jj, ml, st_a, cmax_a, p_b, p_a)
        return steps(2 * jj + 1, ml, st_b, cmax_b, p_a, p_b)

    ml = lax.fori_loop(
        0, i, pair, tuple((jnp.full((1, tq), NEG_BIG, F32), jnp.zeros((1, tq), F32)) for _ in hs))
    ml = steps(2 * i, ml, st_a, cmax_a, p_b, p_a, key0=0)
    ml = steps(2 * i + 1, ml, st_b, cmax_b, p_a, p_b, key0=tk)
    for hh in hs:
        acc = acc_ref[hh] + weighted_values(hh, 2 * i + 1, p_b[hh])
        lanes = slice(hh * HEAD_DIM, (hh + 1) * HEAD_DIM)
        o_ref[:, lanes] = ((acc / ml[hh][1]).T * gate_ref[:, lanes].astype(F32)).astype(o_ref.dtype)


def _fox_attention(q, k, v, gate, c, kaux):
    s, d = q.shape
    heads = d // HEAD_DIM
    tq = _tile(s, 1024)
    hps = ATTN_HEADS_PER_STEP if heads % ATTN_HEADS_PER_STEP == 0 else 1
    kern = functools.partial(_fox_attn_kernel, tq=tq, heads_per_step=hps)
    kv_spec = lambda: pl.BlockSpec((hps, s, HEAD_DIM), lambda h, i: (h, 0, 0))
    return pl.pallas_call(
        kern,
        out_shape=jax.ShapeDtypeStruct((s, d), BF16),
        grid=(heads // hps, s // tq),
        in_specs=[pl.BlockSpec((tq, hps * HEAD_DIM), lambda h, i: (i, h)),
                  kv_spec(), kv_spec(),
                  pl.BlockSpec((s, HEAD_DIM), lambda h, i: (0, 0)),
                  pl.BlockSpec((8, HEAD_DIM), lambda h, i: (i * (tq // 8), 0)),
                  pl.BlockSpec((tq, hps * HEAD_DIM), lambda h, i: (i, h))],
        out_specs=pl.BlockSpec((tq, hps * HEAD_DIM), lambda h, i: (i, h)),
        scratch_shapes=[pltpu.VMEM((hps, tq // 2, tq), F32), pltpu.VMEM((hps, tq // 2, tq), F32),
                        pltpu.VMEM((hps, 1, tq), F32), pltpu.VMEM((hps, 1, tq), F32),
                        pltpu.VMEM((hps, tq // 2, tq), BF16), pltpu.VMEM((hps, tq // 2, tq), BF16),
                        pltpu.VMEM((hps, HEAD_DIM, tq), F32)],
        compiler_params=_params("parallel", "arbitrary"),
        name="fox_attention",
    )(q, k, v, kaux, c, gate)


def _hgrn_level_widths(t):
    ws = []
    w = t // 2
    while w >= HGRN_DIAG:
        ws.append(w)
        w //= 2
    return ws


def _hgrn_sum_masks(t):
    r = jnp.arange(t)[:, None]
    c = jnp.arange(t)[None, :]
    blocks = [c <= r]
    for w in _hgrn_level_widths(t):
        ref = (r // (2 * w)) * (2 * w) + w - 1
        q_side = (r % (2 * w)) >= w
        blocks.append(jnp.where(q_side, (c > ref) & (c <= r), (c > r) & (c <= ref)))
    return jnp.concatenate(blocks, axis=0).astype(BF16)


def _hgrn_scan_kernel(q_ref, k_ref, g_ref, v_ref, gate_ref, gain_ref, masks_ref, ones_ref,
                      o_ref, state_ref, *, t, heads_per_step):
    @pl.when(pl.program_id(1) == 0)
    def _():
        state_ref[...] = jnp.zeros_like(state_ref)

    for hh in range(heads_per_step):
        lanes = slice(hh * HEAD_DIM, (hh + 1) * HEAD_DIM)
        _hgrn_scan_head(q_ref.at[:, lanes], k_ref.at[:, lanes], g_ref.at[:, lanes], v_ref.at[:, lanes],
                        gate_ref.at[:, lanes], gain_ref, masks_ref, ones_ref, o_ref.at[:, lanes],
                        state_ref.at[hh], t)


def _hgrn_scan_head(q_ref, k_ref, g_ref, v_ref, gate_ref, gain_ref, masks_ref, ones_ref, o_ref,
                    state_ref, t):
    q = q_ref[...].astype(F32)
    k = k_ref[...].astype(F32)
    v_bf = v_ref[...]
    v = v_bf.astype(F32)
    g = g_ref[...]

    pieces = jnp.concatenate(_split_bf16(g, 2), axis=1)
    sums2 = jnp.dot(masks_ref[...], pieces, preferred_element_type=F32)
    sums = sums2[:, :HEAD_DIM] + sums2[:, HEAD_DIM:]
    b = sums[0:t]
    b_last = b[t - 1:t]

    st = state_ref[...]
    o = lax.dot_general((q * jnp.exp(b)).astype(BF16), st.astype(BF16), _NT, preferred_element_type=F32)

    rows = lax.broadcasted_iota(jnp.int32, (t, 1), 0)
    scores = jnp.zeros((t, t), F32)
    for lvl, w in enumerate(_hgrn_level_widths(t)):
        fac = jnp.exp(sums[(lvl + 1) * t:(lvl + 2) * t])
        q_side = (rows % (2 * w)) >= w
        a = jnp.where(q_side, q * fac, 0.0).astype(BF16)
        bk = jnp.where(q_side, 0.0, k * fac).astype(BF16)
        sc = lax.dot_general(a, bk, _NT, preferred_element_type=F32)
        if 2 * w < t:
            ri = lax.broadcasted_iota(jnp.int32, (t, t), 0) // (2 * w)
            cj = lax.broadcasted_iota(jnp.int32, (t, t), 1) // (2 * w)
            sc = jnp.where(ri == cj, sc, 0.0)
        scores = scores + sc
    o = o + jnp.dot(scores.astype(BF16), v_bf, preferred_element_type=F32)

    def group_roll(a, dd):
        return pltpu.roll(a.reshape(t // HGRN_DIAG, HGRN_DIAG, HEAD_DIM), dd, 1).reshape(t, HEAD_DIM)

    sub = rows % HGRN_DIAG
    rel = jnp.zeros_like(g)
    xs, vds = [q * k], [v]
    for dd in range(1, HGRN_DIAG):
        rel = rel + (group_roll(g, dd - 1) if dd > 1 else g)
        xs.append(q * group_roll(k, dd) * jnp.exp(jnp.where(sub >= dd, rel, NEG_BIG)))
        vds.append(group_roll(v, dd))
    ones2 = ones_ref[...]
    for dd in range(0, HGRN_DIAG, 2):
        pair = jnp.concatenate([xs[dd], xs[dd + 1]], axis=1).astype(BF16)
        rowsums = jnp.dot(pair, ones2, preferred_element_type=F32)
        o = o + rowsums[:, :HEAD_DIM] * vds[dd] + rowsums[:, HEAD_DIM:] * vds[dd + 1]

    kh = (k * jnp.exp(b_last - b)).astype(BF16)
    state_ref[...] = st * jnp.exp(b_last) + lax.dot_general(v_bf, kh, _TN, preferred_element_type=F32)

    ms = jnp.mean(o * o, axis=-1, keepdims=True)
    y = o * lax.rsqrt(ms + NORM_EPS) * gain_ref[...]
    o_ref[...] = (y * gate_ref[...].astype(F32)).astype(o_ref.dtype)


def _hgrn_scan(q, k, g, v, gate, out_gain):
    s, d = g.shape
    heads = d // HEAD_DIM
    t = _tile(s, HGRN_CHUNK)
    masks = _hgrn_sum_masks(t)
    nm = masks.shape[0]
    hps = HGRN_HEADS_PER_STEP if heads % HGRN_HEADS_PER_STEP == 0 else 1
    kern = functools.partial(_hgrn_scan_kernel, t=t, heads_per_step=hps)
    blk = lambda: pl.BlockSpec((t, hps * HEAD_DIM), lambda h, c: (c, h))
    return pl.pallas_call(
        kern,
        out_shape=jax.ShapeDtypeStruct((s, d), BF16),
        grid=(heads // hps, s // t),
        in_specs=[blk(), blk(), blk(), blk(), blk(),
                  pl.BlockSpec((1, HEAD_DIM), lambda h, c: (0, 0)),
                  pl.BlockSpec((nm, t), lambda h, c: (0, 0)),
                  pl.BlockSpec((2 * HEAD_DIM, 2 * HEAD_DIM), lambda h, c: (0, 0))],
        out_specs=blk(),
        scratch_shapes=[pltpu.VMEM((hps, HEAD_DIM, HEAD_DIM), F32)],
        compiler_params=_params("parallel", "arbitrary"),
        name="hgrn_scan",
    )(q, k, g, v, gate, out_gain.reshape(1, HEAD_DIM).astype(F32), masks,
      jnp.kron(jnp.eye(2, dtype=BF16), jnp.ones((HEAD_DIM, HEAD_DIM), BF16)))


def kernel(x, fox_w_in, fox_w_out, fox_q_gain, fox_k_gain, fox_fgate_bias, hgrn_w_in, hgrn_w_out,
           hgrn_out_gain, hgrn_lb_logits, mixer_norm_gain, mlp_norm_gain, mlp_w_up, mlp_w_down,
           final_norm_gain):
    batch, seq, d = x.shape
    depth = mixer_norm_gain.shape[0]
    fox_w_in_b, fox_w_out_b = fox_w_in.astype(BF16), fox_w_out.astype(BF16)
    hgrn_w_in_b, hgrn_w_out_b = hgrn_w_in.astype(BF16), hgrn_w_out.astype(BF16)
    mlp_w_up_b, mlp_w_down_b = mlp_w_up.astype(BF16), mlp_w_down.astype(BF16)
    q_scale = HEAD_DIM ** -0.5 * LOG2E
    outs = []
    for bi in range(batch):
        h = x[bi]
        for i in range(depth):
            j = i // 2
            n = _rmsnorm(h, mixer_norm_gain[i], BF16)
            if i % 2 == 0:
                q = _in_proj(n, fox_w_in_b, j, 0, "fox_in_proj_q", head_gain=fox_q_gain[j] * q_scale)
                k = _in_proj(n, fox_w_in_b, j, d, "fox_in_proj_k", head_gain=fox_k_gain[j], head_major=True)
                v = _in_proj(n, fox_w_in_b, j, 2 * d, "fox_in_proj_v", head_major=True)
                gate = _in_proj(n, fox_w_in_b, j, 3 * d, "fox_in_proj_gate", act=_sigmoid)
                c, kaux = _fox_forget_cumsum(n, fox_w_in_b[j, :, 4 * d:], fox_fgate_bias[j])
                mix_in = _fox_attention(q, k, v, gate, c, kaux)
                h = _proj_residual(mix_in, fox_w_out_b, j, h, name="mixer_out_proj")
            else:
                q = _in_proj(n, hgrn_w_in_b, j, 0, "hgrn_in_proj_q", act=_silu)
                g, k = _hgrn_in_proj_forget(n, hgrn_w_in_b, j, hgrn_lb_logits, i)
                v = _in_proj(n, hgrn_w_in_b, j, 2 * d, "hgrn_in_proj_v")
                gate = _in_proj(n, hgrn_w_in_b, j, 3 * d, "hgrn_in_proj_gate", act=_silu)
                mix_in = _hgrn_scan(q, k, g, v, gate, hgrn_out_gain[j])
                h = _proj_residual(mix_in, hgrn_w_out_b, j, h, name="mixer_out_proj")
            n = _rmsnorm(h, mlp_norm_gain[i], BF16)
            a = _mlp_up(n, mlp_w_up_b, i)
            h = _proj_residual(a, mlp_w_down_b, i, h, name="mlp_down")
        outs.append(_rmsnorm(h, final_norm_gain, x.dtype))
    return jnp.stack(outs, axis=0)
```

```python
import functools

import jax
import jax.numpy as jnp
from jax import lax
from jax.experimental import pallas as pl
from jax.experimental.pallas import tpu as pltpu

F32 = jnp.float32
BF16 = jnp.bfloat16

HEAD_DIM = 128
NORM_EPS = 1e-6
NEG_BIG = -1e30
VMEM_LIMIT_BYTES = 56 * 1024 * 1024

LOG2E = 1.4426950408889634
AUX_PIECES = 3
AUX_CONST_LANE = 96

ATTN_HEADS_PER_STEP = 2

HGRN_CHUNK = 256
HGRN_HEADS_PER_STEP = 4
HGRN_DIAG = 8

_NT = (((1,), (1,)), ((), ()))
_TN = (((0,), (0,)), ((), ()))


def _tile(n, pref):
    t = min(n, pref)
    while n % t:
        t //= 2
    return t


def _params(*sem):
    return pltpu.CompilerParams(dimension_semantics=sem, vmem_limit_bytes=VMEM_LIMIT_BYTES)


def _split_bf16(a, terms):
    parts = []
    r = a
    for _ in range(terms):
        p = r.astype(BF16)
        parts.append(p)
        r = r - p.astype(F32)
    return parts


def _sigmoid(x):
    return 1.0 / (1.0 + jnp.exp(-x))


def _log_sigmoid(x):
    return jnp.minimum(x, 0.0) - jnp.log(1.0 + jnp.exp(-jnp.abs(x)))


def _rmsnorm_kernel(x_ref, g_ref, o_ref):
    x = x_ref[...]
    ms = jnp.mean(x * x, axis=-1, keepdims=True)
    o_ref[...] = (x * lax.rsqrt(ms + NORM_EPS) * g_ref[...]).astype(o_ref.dtype)


def _rmsnorm(x, gain, out_dtype):
    s, d = x.shape
    tm = _tile(s, 256)
    return pl.pallas_call(
        _rmsnorm_kernel,
        out_shape=jax.ShapeDtypeStruct((s, d), out_dtype),
        grid=(s // tm,),
        in_specs=[pl.BlockSpec((tm, d), lambda i: (i, 0)),
                  pl.BlockSpec((1, d), lambda i: (0, 0))],
        out_specs=pl.BlockSpec((tm, d), lambda i: (i, 0)),
        compiler_params=_params("parallel"),
        name="rmsnorm",
    )(x, gain.reshape(1, d).astype(F32))


def _matmul_kernel(x_ref, w_ref, *refs, nk, n_aux, epilogue):
    aux, outs = refs[:n_aux], refs[n_aux:]
    if nk == 1:
        epilogue(jnp.dot(x_ref[...], w_ref[...], preferred_element_type=F32), aux, outs)
        return
    (h_ref,), (o_ref,) = aux, outs

    @pl.when(pl.program_id(2) == 0)
    def _():
        o_ref[...] = h_ref[...]

    o_ref[...] += jnp.dot(x_ref[...], w_ref[...], preferred_element_type=F32)


def _matmul(x, w, layer, *, tm, tn, tk, n_cols, w_col_block, epilogue, aux, aux_specs, out_shapes, name,
            out_specs=None):
    m, kdim = x.shape
    nk = kdim // tk
    if nk > 1:
        assert epilogue is None and len(aux) == 1 and len(out_shapes) == 1 and out_shapes[0].dtype == F32
    kern = functools.partial(_matmul_kernel, nk=nk, n_aux=len(aux), epilogue=epilogue)
    return pl.pallas_call(
        kern,
        out_shape=out_shapes,
        grid=(m // tm, n_cols // tn, nk),
        in_specs=[pl.BlockSpec((tm, tk), lambda i, j, k: (i, k)),
                  pl.BlockSpec((None, tk, tn), lambda i, j, k: (layer, k, w_col_block(j)))] + list(aux_specs),
        out_specs=out_specs or [pl.BlockSpec((tm, tn), lambda i, j, k: (i, j)) for _ in out_shapes],
        compiler_params=_params("parallel", "parallel", "arbitrary"),
        name=name,
    )(x, w, *aux)


def _silu(x):
    return x * _sigmoid(x)


def _in_proj(n, w, layer, col0, name, *, act=None, head_gain=None, head_major=False):
    s, d = n.shape
    tm, tn = _tile(s, 1024), _tile(d, 512)
    blk0 = col0 // tn
    hpt = tn // HEAD_DIM

    def epilogue(acc, aux, outs):
        (o_ref,) = outs
        for c in range(hpt):
            xc = acc[:, c * HEAD_DIM:(c + 1) * HEAD_DIM]
            if head_gain is not None:
                ms = jnp.mean(xc * xc, axis=-1, keepdims=True)
                xc = xc * lax.rsqrt(ms + NORM_EPS) * aux[0][...]
            elif act is not None:
                xc = act(xc)
            if head_major:
                o_ref[c] = xc.astype(o_ref.dtype)
            else:
                o_ref[:, c * HEAD_DIM:(c + 1) * HEAD_DIM] = xc.astype(o_ref.dtype)

    aux, aux_specs = (), ()
    if head_gain is not None:
        aux = (head_gain.astype(F32).reshape(1, HEAD_DIM),)
        aux_specs = (pl.BlockSpec((1, HEAD_DIM), lambda i, j, k: (0, 0)),)
    if head_major:
        out_shape = jax.ShapeDtypeStruct((d // HEAD_DIM, s, HEAD_DIM), BF16)
        out_specs = [pl.BlockSpec((hpt, tm, HEAD_DIM), lambda i, j, k: (j, i, 0))]
    else:
        out_shape, out_specs = jax.ShapeDtypeStruct((s, d), BF16), None
    (out,) = _matmul(n, w, layer, tm=tm, tn=tn, tk=d, n_cols=d, w_col_block=lambda j: j + blk0,
                     epilogue=epilogue, aux=aux, aux_specs=aux_specs,
                     out_shapes=[out_shape], out_specs=out_specs, name=name)
    return out


def _hgrn_in_proj_forget(n, w_in, w_layer, lb_logits, layer):
    s, d = n.shape
    depth = lb_logits.shape[0]
    tm, tn = _tile(s, 1024), _tile(d, 512)
    nq = d // tn

    def epilogue(acc, aux, outs):
        (lb_ref,) = aux
        g_ref, k_ref = outs
        z = lb_ref[...]
        e = jnp.exp(z - jnp.max(z, axis=0, keepdims=True))
        p = e / jnp.sum(e, axis=0, keepdims=True)
        lb = jnp.zeros((1, tn), F32)
        for r in range(1, layer + 1):
            lb = lb + p[r:r + 1, :]
        f = lb + (1.0 - lb) * _sigmoid(acc)
        g_ref[...] = jnp.log(f)
        k_ref[...] = (1.0 - f).astype(k_ref.dtype)

    g, k = _matmul(n, w_in, w_layer, tm=tm, tn=tn, tk=d, n_cols=d, w_col_block=lambda j: j + nq,
                   epilogue=epilogue, aux=(lb_logits.astype(F32),),
                   aux_specs=(pl.BlockSpec((depth, tn), lambda i, j, k: (0, j)),),
                   out_shapes=[jax.ShapeDtypeStruct((s, d), F32), jax.ShapeDtypeStruct((s, d), BF16)],
                   name="hgrn_in_proj_forget")
    return g, k


def _proj_residual(x, w, layer, h, *, name):
    s, kdim = x.shape
    d = w.shape[2]
    tm = _tile(s, 1024)
    if kdim <= 4096:
        tn, tk = _tile(d, 512), kdim

        def epilogue(acc, aux, outs):
            (h_ref,) = aux
            (o_ref,) = outs
            o_ref[...] = h_ref[...] + acc
    else:
        tn, tk, epilogue = _tile(d, 2048), 1024, None

    (out,) = _matmul(x, w, layer, tm=tm, tn=tn, tk=tk, n_cols=d, w_col_block=lambda j: j,
                     epilogue=epilogue, aux=(h,),
                     aux_specs=(pl.BlockSpec((tm, tn), lambda i, j, k: (i, j)),),
                     out_shapes=[jax.ShapeDtypeStruct((s, d), F32)], name=name)
    return out


def _mlp_up(n, w_up, layer):
    s, d = n.shape
    f = w_up.shape[2]
    tm, tn = _tile(s, 1024), _tile(f, 512)

    def epilogue(acc, aux, outs):
        (o_ref,) = outs
        r = jnp.maximum(acc, 0.0)
        o_ref[...] = (r * r).astype(o_ref.dtype)

    (out,) = _matmul(n, w_up, layer, tm=tm, tn=tn, tk=d, n_cols=f, w_col_block=lambda j: j,
                     epilogue=epilogue, aux=(), aux_specs=(),
                     out_shapes=[jax.ShapeDtypeStruct((s, f), BF16)], name="mlp_up")
    return out


def _fox_forget_kernel(x_ref, wf_ref, bias_ref, tri_ref, perm_ref, c_ref, kaux_ref, carry_ref):
    i = pl.program_id(0)

    @pl.when(i == 0)
    def _():
        carry_ref[...] = jnp.zeros_like(carry_ref)

    fz = jnp.dot(x_ref[...], wf_ref[...], preferred_element_type=F32)
    lf = _log_sigmoid(fz + bias_ref[...]) * LOG2E
    tri = tri_ref[...]
    cs = jnp.zeros_like(lf)
    for part in _split_bf16(lf, 3):
        cs = cs + jnp.dot(tri, part, preferred_element_type=F32)
    c = cs + carry_ref[...]
    c_ref[...] = c
    carry_ref[...] = c[c.shape[0] - 1:, :]

    pieces = jnp.concatenate(_split_bf16(c, AUX_PIECES), axis=1)
    lane = lax.broadcasted_iota(jnp.int32, (1, HEAD_DIM), 1)
    const = jnp.where(jnp.logical_and(lane >= AUX_CONST_LANE, lane < AUX_CONST_LANE + AUX_PIECES), 1.0, 0.0)
    kaux = jnp.dot(pieces, perm_ref[...], preferred_element_type=F32) + const
    kaux_ref[...] = kaux.astype(kaux_ref.dtype)


def _fox_forget_cumsum(n, w_f, bias):
    s, d = n.shape
    heads = w_f.shape[1]
    assert AUX_PIECES * heads <= AUX_CONST_LANE
    tm = _tile(s, 512)
    r = jnp.arange(tm)
    tri = (r[None, :] <= r[:, None]).astype(BF16)
    wf = jnp.zeros((d, HEAD_DIM), BF16).at[:, :heads].set(w_f)
    b = jnp.zeros((1, HEAD_DIM), F32).at[0, :heads].set(bias.astype(F32))
    hh = jnp.arange(heads)
    perm = jnp.zeros((AUX_PIECES * HEAD_DIM, HEAD_DIM), BF16)
    for p in range(AUX_PIECES):
        perm = perm.at[p * HEAD_DIM + hh, AUX_PIECES * hh + p].set(-1.0)
    return pl.pallas_call(
        _fox_forget_kernel,
        out_shape=[jax.ShapeDtypeStruct((s, HEAD_DIM), F32), jax.ShapeDtypeStruct((s, HEAD_DIM), BF16)],
        grid=(s // tm,),
        in_specs=[pl.BlockSpec((tm, d), lambda i: (i, 0)),
                  pl.BlockSpec((d, HEAD_DIM), lambda i: (0, 0)),
                  pl.BlockSpec((1, HEAD_DIM), lambda i: (0, 0)),
                  pl.BlockSpec((tm, tm), lambda i: (0, 0)),
                  pl.BlockSpec((AUX_PIECES * HEAD_DIM, HEAD_DIM), lambda i: (0, 0))],
        out_specs=[pl.BlockSpec((tm, HEAD_DIM), lambda i: (i, 0)),
                   pl.BlockSpec((tm, HEAD_DIM), lambda i: (i, 0))],
        scratch_shapes=[pltpu.VMEM((1, HEAD_DIM), F32)],
        compiler_params=_params("arbitrary"),
        name="fox_forget_cumsum",
    )(n, wf, b, tri, perm)


def _fox_attn_kernel(q_ref, k_ref, v_ref, kaux_ref, c_ref, gate_ref, o_ref,
                     st_a, st_b, cmax_a, cmax_b, p_a, p_b, acc_ref, *, tq, heads_per_step):
    i = pl.program_id(1)
    tk = tq // 2
    lane = lax.broadcasted_iota(jnp.int32, (1, HEAD_DIM), 1)
    hs = range(heads_per_step)

    q_aug = []
    for hh in hs:
        h = pl.program_id(0) * heads_per_step + hh
        c0 = jnp.sum(jnp.where(lane == h, c_ref[0:1, :], 0.0), axis=1, keepdims=True)
        qaux = jnp.where(jnp.logical_and(lane >= AUX_PIECES * h, lane < AUX_PIECES * (h + 1)), 1.0, 0.0)
        for p, piece in enumerate(_split_bf16(c0, AUX_PIECES)):
            qaux = jnp.where(lane == AUX_CONST_LANE + p, piece.astype(F32), qaux)
        q_aug.append(jnp.concatenate(
            [q_ref[:, hh * HEAD_DIM:(hh + 1) * HEAD_DIM],
             jnp.broadcast_to(qaux.astype(BF16), (tq, HEAD_DIM))], axis=1))

    def logits(hh, c):
        row0 = pl.multiple_of(c * tk, tk)
        k_aug = jnp.concatenate([k_ref[hh, pl.ds(row0, tk), :], kaux_ref[pl.ds(row0, tk), :]], axis=1)
        return lax.dot_general(k_aug, q_aug[hh], _NT, preferred_element_type=F32)

    def weighted_values(hh, c, p):
        row0 = pl.multiple_of(c * tk, tk)
        return lax.dot_general(v_ref[hh, pl.ds(row0, tk), :], p, _TN, preferred_element_type=F32)

    def fill(hh, c, st_buf, cmax_buf):
        st = logits(hh, c)
        st_buf[hh] = st
        cmax_buf[hh] = jnp.max(st, axis=0, keepdims=True)

    def step(hh, c, m, l, st_cur, cmax_cur, p_prev, p_cur, key0=None):
        acc = acc_ref[hh] + weighted_values(hh, jnp.maximum(c - 1, 0), p_prev[hh])
        st = st_cur[hh]
        if key0 is None:
            cmax = cmax_cur[hh]
            fill(hh, c + 2, st_cur, cmax_cur)
        else:
            keys = lax.broadcasted_iota(jnp.int32, (tk, tq), 0) + key0
            queries = lax.broadcasted_iota(jnp.int32, (tk, tq), 1)
            st = jnp.where(keys <= queries, st, NEG_BIG)
            cmax = jnp.max(st, axis=0, keepdims=True)
        m_new = jnp.maximum(m, cmax)
        alpha = jnp.exp2(m - m_new)
        p = jnp.exp2(st - m_new)
        l_new = alpha * l + jnp.sum(p, axis=0, keepdims=True)
        p_cur[hh] = p.astype(BF16)
        acc_ref[hh] = alpha * acc
        return m_new, l_new

    def steps(c, ml, st_cur, cmax_cur, p_prev, p_cur, key0=None):
        return tuple(step(hh, c, *ml[hh], st_cur, cmax_cur, p_prev, p_cur, key0) for hh in hs)

    for hh in hs:
        fill(hh, 0, st_a, cmax_a)
        fill(hh, 1, st_b, cmax_b)
    p_b[...] = jnp.zeros_like(p_b)
    acc_ref[...] = jnp.zeros_like(acc_ref)

    def pair(jj, ml):
        ml = steps(2 * jj, ml, st_a, cmax_a, p_b, p_a)
        return steps(2 * jj + 1, ml, st_b, cmax_b, p_a, p_b)

    ml = lax.fori_loop(
        0, i, pair, tuple((jnp.full((1, tq), NEG_BIG, F32), jnp.zeros((1, tq), F32)) for _ in hs))
    ml = steps(2 * i, ml, st_a, cmax_a, p_b, p_a, key0=0)
    ml = steps(2 * i + 1, ml, st_b, cmax_b, p_a, p_b, key0=tk)
    for hh in hs:
        acc = acc_ref[hh] + weighted_values(hh, 2 * i + 1, p_b[hh])
        lanes = slice(hh * HEAD_DIM, (hh + 1) * HEAD_DIM)
        o_ref[:, lanes] = ((acc / ml[hh][1]).T * gate_ref[:, lanes].astype(F32)).astype(o_ref.dtype)


def _fox_attention(q, k, v, gate, c, kaux):
    s, d = q.shape
    heads = d // HEAD_DIM
    tq = _tile(s, 1024)
    hps = ATTN_HEADS_PER_STEP if heads % ATTN_HEADS_PER_STEP == 0 else 1
    kern = functools.partial(_fox_attn_kernel, tq=tq, heads_per_step=hps)
    once = pl.Buffered(1)
    kv_spec = lambda: pl.BlockSpec((hps, s, HEAD_DIM), lambda h, i: (h, 0, 0), pipeline_mode=once)
    return pl.pallas_call(
        kern,
        out_shape=jax.ShapeDtypeStruct((s, d), BF16),
        grid=(heads // hps, s // tq),
        in_specs=[pl.BlockSpec((tq, hps * HEAD_DIM), lambda h, i: (i, h)),
                  kv_spec(), kv_spec(),
                  pl.BlockSpec((s, HEAD_DIM), lambda h, i: (0, 0), pipeline_mode=once),
                  pl.BlockSpec((8, HEAD_DIM), lambda h, i: (i * (tq // 8), 0)),
                  pl.BlockSpec((tq, hps * HEAD_DIM), lambda h, i: (i, h))],
        out_specs=pl.BlockSpec((tq, hps * HEAD_DIM), lambda h, i: (i, h)),
        scratch_shapes=[pltpu.VMEM((hps, tq // 2, tq), F32), pltpu.VMEM((hps, tq // 2, tq), F32),
                        pltpu.VMEM((hps, 1, tq), F32), pltpu.VMEM((hps, 1, tq), F32),
                        pltpu.VMEM((hps, tq // 2, tq), BF16), pltpu.VMEM((hps, tq // 2, tq), BF16),
                        pltpu.VMEM((hps, HEAD_DIM, tq), F32)],
        compiler_params=_params("parallel", "arbitrary"),
        name="fox_attention",
    )(q, k, v, kaux, c, gate)


def _hgrn_level_widths(t):
    ws = []
    w = t // 2
    while w >= HGRN_DIAG:
        ws.append(w)
        w //= 2
    return ws


def _hgrn_sum_masks(t):
    r = jnp.arange(t)[:, None]
    c = jnp.arange(t)[None, :]
    blocks = [c <= r]
    for w in _hgrn_level_widths(t):
        ref = (r // (2 * w)) * (2 * w) + w - 1
        q_side = (r % (2 * w)) >= w
        blocks.append(jnp.where(q_side, (c > ref) & (c <= r), (c > r) & (c <= ref)))
    return jnp.concatenate(blocks, axis=0).astype(BF16)


def _hgrn_scan_kernel(q_ref, k_ref, g_ref, v_ref, gate_ref, gain_ref, masks_ref, ones_ref,
                      o_ref, state_ref, *, t, heads_per_step):
    @pl.when(pl.program_id(1) == 0)
    def _():
        state_ref[...] = jnp.zeros_like(state_ref)

    for hh in range(heads_per_step):
        lanes = slice(hh * HEAD_DIM, (hh + 1) * HEAD_DIM)
        _hgrn_scan_head(q_ref.at[:, lanes], k_ref.at[:, lanes], g_ref.at[:, lanes], v_ref.at[:, lanes],
                        gate_ref.at[:, lanes], gain_ref, masks_ref, ones_ref, o_ref.at[:, lanes],
                        state_ref.at[hh], t)


def _hgrn_scan_head(q_ref, k_ref, g_ref, v_ref, gate_ref, gain_ref, masks_ref, ones_ref, o_ref,
                    state_ref, t):
    q = q_ref[...].astype(F32)
    k = k_ref[...].astype(F32)
    v_bf = v_ref[...]
    v = v_bf.astype(F32)
    g = g_ref[...]

    pieces = jnp.concatenate(_split_bf16(g, 2), axis=1)
    sums2 = jnp.dot(masks_ref[...], pieces, preferred_element_type=F32)
    sums = sums2[:, :HEAD_DIM] + sums2[:, HEAD_DIM:]
    b = sums[0:t]
    b_last = b[t - 1:t]

    st = state_ref[...]
    o = lax.dot_general((q * jnp.exp(b)).astype(BF16), st.astype(BF16), _NT, preferred_element_type=F32)

    rows = lax.broadcasted_iota(jnp.int32, (t, 1), 0)
    scores = jnp.zeros((t, t), F32)
    for lvl, w in enumerate(_hgrn_level_widths(t)):
        fac = jnp.exp(sums[(lvl + 1) * t:(lvl + 2) * t])
        q_side = (rows % (2 * w)) >= w
        a = jnp.where(q_side, q * fac, 0.0).astype(BF16)
        bk = jnp.where(q_side, 0.0, k * fac).astype(BF16)
        sc = lax.dot_general(a, bk, _NT, preferred_element_type=F32)
        if 2 * w < t:
            ri = lax.broadcasted_iota(jnp.int32, (t, t), 0) // (2 * w)
            cj = lax.broadcasted_iota(jnp.int32, (t, t), 1) // (2 * w)
            sc = jnp.where(ri == cj, sc, 0.0)
        scores = scores + sc
    o = o + jnp.dot(scores.astype(BF16), v_bf, preferred_element_type=F32)

    def group_roll(a, dd):
        return pltpu.roll(a.reshape(t // HGRN_DIAG, HGRN_DIAG, HEAD_DIM), dd, 1).reshape(t, HEAD_DIM)

    sub = rows % HGRN_DIAG
    rel = jnp.zeros_like(g)
    xs, vds = [q * k], [v]
    for dd in range(1, HGRN_DIAG):
        rel = rel + (group_roll(g, dd - 1) if dd > 1 else g)
        xs.append(q * group_roll(k, dd) * jnp.exp(jnp.where(sub >= dd, rel, NEG_BIG)))
        vds.append(group_roll(v, dd))
    ones2 = ones_ref[...]
    for dd in range(0, HGRN_DIAG, 2):
        pair = jnp.concatenate([xs[dd], xs[dd + 1]], axis=1).astype(BF16)
        rowsums = jnp.dot(pair, ones2, preferred_element_type=F32)
        o = o + rowsums[:, :HEAD_DIM] * vds[dd] + rowsums[:, HEAD_DIM:] * vds[dd + 1]

    kh = (k * jnp.exp(b_last - b)).astype(BF16)
    state_ref[...] = st * jnp.exp(b_last) + lax.dot_general(v_bf, kh, _TN, preferred_element_type=F32)

    ms = jnp.mean(o * o, axis=-1, keepdims=True)
    y = o * lax.rsqrt(ms + NORM_EPS) * gain_ref[...]
    o_ref[...] = (y * gate_ref[...].astype(F32)).astype(o_ref.dtype)


def _hgrn_scan(q, k, g, v, gate, out_gain):
    s, d = g.shape
    heads = d // HEAD_DIM
    t = _tile(s, HGRN_CHUNK)
    masks = _hgrn_sum_masks(t)
    nm = masks.shape[0]
    hps = HGRN_HEADS_PER_STEP if heads % HGRN_HEADS_PER_STEP == 0 else 1
    kern = functools.partial(_hgrn_scan_kernel, t=t, heads_per_step=hps)
    blk = lambda: pl.BlockSpec((t, hps * HEAD_DIM), lambda h, c: (c, h))
    return pl.pallas_call(
        kern,
        out_shape=jax.ShapeDtypeStruct((s, d), BF16),
        grid=(heads // hps, s // t),
        in_specs=[blk(), blk(), blk(), blk(), blk(),
                  pl.BlockSpec((1, HEAD_DIM), lambda h, c: (0, 0)),
                  pl.BlockSpec((nm, t), lambda h, c: (0, 0)),
                  pl.BlockSpec((2 * HEAD_DIM, 2 * HEAD_DIM), lambda h, c: (0, 0))],
        out_specs=blk(),
        scratch_shapes=[pltpu.VMEM((hps, HEAD_DIM, HEAD_DIM), F32)],
        compiler_params=_params("parallel", "arbitrary"),
        name="hgrn_scan",
    )(q, k, g, v, gate, out_gain.reshape(1, HEAD_DIM).astype(F32), masks,
      jnp.kron(jnp.eye(2, dtype=BF16), jnp.ones((HEAD_DIM, HEAD_DIM), BF16)))


def kernel(x, fox_w_in, fox_w_out, fox_q_gain, fox_k_gain, fox_fgate_bias, hgrn_w_in, hgrn_w_out,
           hgrn_out_gain, hgrn_lb_logits, mixer_norm_gain, mlp_norm_gain, mlp_w_up, mlp_w_down,
           final_norm_gain):
    batch, seq, d = x.shape
    depth = mixer_norm_gain.shape[0]
    fox_w_in_b, fox_w_out_b = fox_w_in.astype(BF16), fox_w_out.astype(BF16)
    hgrn_w_in_b, hgrn_w_out_b = hgrn_w_in.astype(BF16), hgrn_w_out.astype(BF16)
    mlp_w_up_b, mlp_w_down_b = mlp_w_up.astype(BF16), mlp_w_down.astype(BF16)
    q_scale = HEAD_DIM ** -0.5 * LOG2E
    outs = []
    for bi in range(batch):
        h = x[bi]
        for i in range(depth):
            j = i // 2
            n = _rmsnorm(h, mixer_norm_gain[i], BF16)
            if i % 2 == 0:
                q = _in_proj(n, fox_w_in_b, j, 0, "fox_in_proj_q", head_gain=fox_q_gain[j] * q_scale)
                k = _in_proj(n, fox_w_in_b, j, d, "fox_in_proj_k", head_gain=fox_k_gain[j], head_major=True)
                v = _in_proj(n, fox_w_in_b, j, 2 * d, "fox_in_proj_v", head_major=True)
                gate = _in_proj(n, fox_w_in_b, j, 3 * d, "fox_in_proj_gate", act=_sigmoid)
                c, kaux = _fox_forget_cumsum(n, fox_w_in_b[j, :, 4 * d:], fox_fgate_bias[j])
                mix_in = _fox_attention(q, k, v, gate, c, kaux)
                h = _proj_residual(mix_in, fox_w_out_b, j, h, name="mixer_out_proj")
            else:
                q = _in_proj(n, hgrn_w_in_b, j, 0, "hgrn_in_proj_q", act=_silu)
                g, k = _hgrn_in_proj_forget(n, hgrn_w_in_b, j, hgrn_lb_logits, i)
                v = _in_proj(n, hgrn_w_in_b, j, 2 * d, "hgrn_in_proj_v")
                gate = _in_proj(n, hgrn_w_in_b, j, 3 * d, "hgrn_in_proj_gate", act=_silu)
                mix_in = _hgrn_scan(q, k, g, v, gate, hgrn_out_gain[j])
                h = _proj_residual(mix_in, hgrn_w_out_b, j, h, name="mixer_out_proj")
            n = _rmsnorm(h, mlp_norm_gain[i], BF16)
            a = _mlp_up(n, mlp_w_up_b, i)
            h = _proj_residual(a, mlp_w_down_b, i, h, name="mlp_down")
        outs.append(_rmsnorm(h, final_norm_gain, x.dtype))
    return jnp.stack(outs, axis=0)
```

```python
import functools

import jax
import jax.numpy as jnp
from jax import lax
from jax.experimental import pallas as pl
from jax.experimental.pallas import tpu as pltpu

F32 = jnp.float32
BF16 = jnp.bfloat16

HEAD_DIM = 128
NORM_EPS = 1e-6
NEG_BIG = -1e30
VMEM_LIMIT_BYTES = 56 * 1024 * 1024

LOG2E = 1.4426950408889634
AUX_PIECES = 3
AUX_CONST_LANE = 96

ATTN_HEADS_PER_STEP = 2

HGRN_CHUNK = 256
HGRN_HEADS_PER_STEP = 4
HGRN_DIAG = 8

_NT = (((1,), (1,)), ((), ()))
_TN = (((0,), (0,)), ((), ()))


def _tile(n, pref):
    t = min(n, pref)
    while n % t:
        t //= 2
    return t


def _params(*sem):
    return pltpu.CompilerParams(dimension_semantics=sem, vmem_limit_bytes=VMEM_LIMIT_BYTES)


def _split_bf16(a, terms):
    parts = []
    r = a
    for _ in range(terms):
        p = r.astype(BF16)
        parts.append(p)
        r = r - p.astype(F32)
    return parts


def _sigmoid(x):
    return 1.0 / (1.0 + jnp.exp(-x))


def _log_sigmoid(x):
    return jnp.minimum(x, 0.0) - jnp.log(1.0 + jnp.exp(-jnp.abs(x)))


def _rmsnorm_kernel(x_ref, g_ref, o_ref):
    x = x_ref[...]
    ms = jnp.mean(x * x, axis=-1, keepdims=True)
    o_ref[...] = (x * lax.rsqrt(ms + NORM_EPS) * g_ref[...]).astype(o_ref.dtype)


def _rmsnorm(x, gain, out_dtype):
    s, d = x.shape
    tm = _tile(s, 256)
    return pl.pallas_call(
        _rmsnorm_kernel,
        out_shape=jax.ShapeDtypeStruct((s, d), out_dtype),
        grid=(s // tm,),
        in_specs=[pl.BlockSpec((tm, d), lambda i: (i, 0)),
                  pl.BlockSpec((1, d), lambda i: (0, 0))],
        out_specs=pl.BlockSpec((tm, d), lambda i: (i, 0)),
        compiler_params=_params("parallel"),
        name="rmsnorm",
    )(x, gain.reshape(1, d).astype(F32))


def _matmul_kernel(x_ref, w_ref, *refs, nk, n_aux, epilogue):
    aux, outs = refs[:n_aux], refs[n_aux:]
    if nk == 1:
        if len(x_ref.shape) == 3:
            x = jnp.concatenate([x_ref[hd] for hd in range(x_ref.shape[0])], axis=1)
        else:
            x = x_ref[...]
        epilogue(jnp.dot(x, w_ref[...], preferred_element_type=F32), aux, outs)
        return
    (h_ref,), (o_ref,) = aux, outs

    @pl.when(pl.program_id(2) == 0)
    def _():
        o_ref[...] = h_ref[...]

    o_ref[...] += jnp.dot(x_ref[...], w_ref[...], preferred_element_type=F32)


def _matmul(x, w, layer, *, tm, tn, tk, n_cols, w_col_block, epilogue, aux, aux_specs, out_shapes, name,
            out_specs=None):
    if x.ndim == 3:
        heads, m, _ = x.shape
        kdim = heads * HEAD_DIM
        assert tk == kdim
        x_spec = pl.BlockSpec((heads, tm, HEAD_DIM), lambda i, j, k: (0, i, 0))
    else:
        m, kdim = x.shape
        x_spec = pl.BlockSpec((tm, tk), lambda i, j, k: (i, k))
    nk = kdim // tk
    if nk > 1:
        assert epilogue is None and len(aux) == 1 and len(out_shapes) == 1 and out_shapes[0].dtype == F32
    kern = functools.partial(_matmul_kernel, nk=nk, n_aux=len(aux), epilogue=epilogue)
    return pl.pallas_call(
        kern,
        out_shape=out_shapes,
        grid=(m // tm, n_cols // tn, nk),
        in_specs=[x_spec,
                  pl.BlockSpec((None, tk, tn), lambda i, j, k: (layer, k, w_col_block(j)))] + list(aux_specs),
        out_specs=out_specs or [pl.BlockSpec((tm, tn), lambda i, j, k: (i, j)) for _ in out_shapes],
        compiler_params=_params("parallel", "parallel", "arbitrary"),
        name=name,
    )(x, w, *aux)


def _silu(x):
    return x * _sigmoid(x)


def _in_proj(n, w, layer, col0, name, *, act=None, head_gain=None, head_major=False):
    s, d = n.shape
    tm, tn = _tile(s, 1024), _tile(d, 512)
    blk0 = col0 // tn
    hpt = tn // HEAD_DIM

    def epilogue(acc, aux, outs):
        (o_ref,) = outs
        for c in range(hpt):
            xc = acc[:, c * HEAD_DIM:(c + 1) * HEAD_DIM]
            if head_gain is not None:
                ms = jnp.mean(xc * xc, axis=-1, keepdims=True)
                xc = xc * lax.rsqrt(ms + NORM_EPS) * aux[0][...]
            elif act is not None:
                xc = act(xc)
            if head_major:
                o_ref[c] = xc.astype(o_ref.dtype)
            else:
                o_ref[:, c * HEAD_DIM:(c + 1) * HEAD_DIM] = xc.astype(o_ref.dtype)

    aux, aux_specs = (), ()
    if head_gain is not None:
        aux = (head_gain.astype(F32).reshape(1, HEAD_DIM),)
        aux_specs = (pl.BlockSpec((1, HEAD_DIM), lambda i, j, k: (0, 0)),)
    if head_major:
        out_shape = jax.ShapeDtypeStruct((d // HEAD_DIM, s, HEAD_DIM), BF16)
        out_specs = [pl.BlockSpec((hpt, tm, HEAD_DIM), lambda i, j, k: (j, i, 0))]
    else:
        out_shape, out_specs = jax.ShapeDtypeStruct((s, d), BF16), None
    (out,) = _matmul(n, w, layer, tm=tm, tn=tn, tk=d, n_cols=d, w_col_block=lambda j: j + blk0,
                     epilogue=epilogue, aux=aux, aux_specs=aux_specs,
                     out_shapes=[out_shape], out_specs=out_specs, name=name)
    return out


def _hgrn_in_proj_forget(n, w_in, w_layer, lb_logits, layer):
    s, d = n.shape
    depth = lb_logits.shape[0]
    tm, tn = _tile(s, 1024), _tile(d, 512)
    nq = d // tn

    def epilogue(acc, aux, outs):
        (lb_ref,) = aux
        g_ref, k_ref = outs
        z = lb_ref[...]
        e = jnp.exp(z - jnp.max(z, axis=0, keepdims=True))
        p = e / jnp.sum(e, axis=0, keepdims=True)
        lb = jnp.zeros((1, tn), F32)
        for r in range(1, layer + 1):
            lb = lb + p[r:r + 1, :]
        f = lb + (1.0 - lb) * _sigmoid(acc)
        g_ref[...] = jnp.log(f)
        k_ref[...] = (1.0 - f).astype(k_ref.dtype)

    g, k = _matmul(n, w_in, w_layer, tm=tm, tn=tn, tk=d, n_cols=d, w_col_block=lambda j: j + nq,
                   epilogue=epilogue, aux=(lb_logits.astype(F32),),
                   aux_specs=(pl.BlockSpec((depth, tn), lambda i, j, k: (0, j)),),
                   out_shapes=[jax.ShapeDtypeStruct((s, d), F32), jax.ShapeDtypeStruct((s, d), BF16)],
                   name="hgrn_in_proj_forget")
    return g, k


def _proj_residual(x, w, layer, h, *, name):
    s, kdim = (x.shape[1], x.shape[0] * HEAD_DIM) if x.ndim == 3 else x.shape
    d = w.shape[2]
    tm = _tile(s, 1024)
    if kdim <= 4096:
        tn, tk = _tile(d, 512), kdim

        def epilogue(acc, aux, outs):
            (h_ref,) = aux
            (o_ref,) = outs
            o_ref[...] = h_ref[...] + acc
    else:
        tn, tk, epilogue = _tile(d, 2048), 1024, None

    (out,) = _matmul(x, w, layer, tm=tm, tn=tn, tk=tk, n_cols=d, w_col_block=lambda j: j,
                     epilogue=epilogue, aux=(h,),
                     aux_specs=(pl.BlockSpec((tm, tn), lambda i, j, k: (i, j)),),
                     out_shapes=[jax.ShapeDtypeStruct((s, d), F32)], name=name)
    return out


def _mlp_up(n, w_up, layer):
    s, d = n.shape
    f = w_up.shape[2]
    tm, tn = _tile(s, 1024), _tile(f, 512)

    def epilogue(acc, aux, outs):
        (o_ref,) = outs
        r = jnp.maximum(acc, 0.0)
        o_ref[...] = (r * r).astype(o_ref.dtype)

    (out,) = _matmul(n, w_up, layer, tm=tm, tn=tn, tk=d, n_cols=f, w_col_block=lambda j: j,
                     epilogue=epilogue, aux=(), aux_specs=(),
                     out_shapes=[jax.ShapeDtypeStruct((s, f), BF16)], name="mlp_up")
    return out


def _fox_forget_kernel(x_ref, wf_ref, bias_ref, tri_ref, perm_ref, c_ref, kaux_ref, carry_ref):
    i = pl.program_id(0)

    @pl.when(i == 0)
    def _():
        carry_ref[...] = jnp.zeros_like(carry_ref)

    fz = jnp.dot(x_ref[...], wf_ref[...], preferred_element_type=F32)
    lf = _log_sigmoid(fz + bias_ref[...]) * LOG2E
    tri = tri_ref[...]
    cs = jnp.zeros_like(lf)
    for part in _split_bf16(lf, 3):
        cs = cs + jnp.dot(tri, part, preferred_element_type=F32)
    c = cs + carry_ref[...]
    c_ref[...] = c
    carry_ref[...] = c[c.shape[0] - 1:, :]

    pieces = jnp.concatenate(_split_bf16(c, AUX_PIECES), axis=1)
    lane = lax.broadcasted_iota(jnp.int32, (1, HEAD_DIM), 1)
    const = jnp.where(jnp.logical_and(lane >= AUX_CONST_LANE, lane < AUX_CONST_LANE + AUX_PIECES), 1.0, 0.0)
    kaux = jnp.dot(pieces, perm_ref[...], preferred_element_type=F32) + const
    kaux_ref[...] = kaux.astype(kaux_ref.dtype)


def _fox_forget_cumsum(n, w_f, bias):
    s, d = n.shape
    heads = w_f.shape[1]
    assert AUX_PIECES * heads <= AUX_CONST_LANE
    tm = _tile(s, 512)
    r = jnp.arange(tm)
    tri = (r[None, :] <= r[:, None]).astype(BF16)
    wf = jnp.zeros((d, HEAD_DIM), BF16).at[:, :heads].set(w_f)
    b = jnp.zeros((1, HEAD_DIM), F32).at[0, :heads].set(bias.astype(F32))
    hh = jnp.arange(heads)
    perm = jnp.zeros((AUX_PIECES * HEAD_DIM, HEAD_DIM), BF16)
    for p in range(AUX_PIECES):
        perm = perm.at[p * HEAD_DIM + hh, AUX_PIECES * hh + p].set(-1.0)
    return pl.pallas_call(
        _fox_forget_kernel,
        out_shape=[jax.ShapeDtypeStruct((s, HEAD_DIM), F32), jax.ShapeDtypeStruct((s, HEAD_DIM), BF16)],
        grid=(s // tm,),
        in_specs=[pl.BlockSpec((tm, d), lambda i: (i, 0)),
                  pl.BlockSpec((d, HEAD_DIM), lambda i: (0, 0)),
                  pl.BlockSpec((1, HEAD_DIM), lambda i: (0, 0)),
                  pl.BlockSpec((tm, tm), lambda i: (0, 0)),
                  pl.BlockSpec((AUX_PIECES * HEAD_DIM, HEAD_DIM), lambda i: (0, 0))],
        out_specs=[pl.BlockSpec((tm, HEAD_DIM), lambda i: (i, 0)),
                   pl.BlockSpec((tm, HEAD_DIM), lambda i: (i, 0))],
        scratch_shapes=[pltpu.VMEM((1, HEAD_DIM), F32)],
        compiler_params=_params("arbitrary"),
        name="fox_forget_cumsum",
    )(n, wf, b, tri, perm)


def _fox_attn_kernel(q_ref, k_ref, v_ref, kaux_ref, c_ref, gate_ref, o_ref,
                     st_a, st_b, cmax_a, cmax_b, p_a, p_b, acc_ref, *, tq, heads_per_step):
    i = pl.program_id(1)
    tk = tq // 2
    lane = lax.broadcasted_iota(jnp.int32, (1, HEAD_DIM), 1)
    hs = range(heads_per_step)

    q_aug = []
    for hh in hs:
        h = pl.program_id(0) * heads_per_step + hh
        c0 = jnp.sum(jnp.where(lane == h, c_ref[0:1, :], 0.0), axis=1, keepdims=True)
        qaux = jnp.where(jnp.logical_and(lane >= AUX_PIECES * h, lane < AUX_PIECES * (h + 1)), 1.0, 0.0)
        for p, piece in enumerate(_split_bf16(c0, AUX_PIECES)):
            qaux = jnp.where(lane == AUX_CONST_LANE + p, piece.astype(F32), qaux)
        q_aug.append(jnp.concatenate(
            [q_ref[hh], jnp.broadcast_to(qaux.astype(BF16), (tq, HEAD_DIM))], axis=1))

    def logits(hh, c):
        row0 = pl.multiple_of(c * tk, tk)
        k_aug = jnp.concatenate([k_ref[hh, pl.ds(row0, tk), :], kaux_ref[pl.ds(row0, tk), :]], axis=1)
        return lax.dot_general(k_aug, q_aug[hh], _NT, preferred_element_type=F32)

    def weighted_values(hh, c, p):
        row0 = pl.multiple_of(c * tk, tk)
        return lax.dot_general(v_ref[hh, pl.ds(row0, tk), :], p, _TN, preferred_element_type=F32)

    def fill(hh, c, st_buf, cmax_buf):
        st = logits(hh, c)
        st_buf[hh] = st
        cmax_buf[hh] = jnp.max(st, axis=0, keepdims=True)

    def step(hh, c, m, l, st_cur, cmax_cur, p_prev, p_cur, key0=None):
        acc = acc_ref[hh] + weighted_values(hh, jnp.maximum(c - 1, 0), p_prev[hh])
        st = st_cur[hh]
        if key0 is None:
            cmax = cmax_cur[hh]
            fill(hh, c + 2, st_cur, cmax_cur)
        else:
            keys = lax.broadcasted_iota(jnp.int32, (tk, tq), 0) + key0
            queries = lax.broadcasted_iota(jnp.int32, (tk, tq), 1)
            st = jnp.where(keys <= queries, st, NEG_BIG)
            cmax = jnp.max(st, axis=0, keepdims=True)
        m_new = jnp.maximum(m, cmax)
        alpha = jnp.exp2(m - m_new)
        p = jnp.exp2(st - m_new)
        l_new = alpha * l + jnp.sum(p, axis=0, keepdims=True)
        p_cur[hh] = p.astype(BF16)
        acc_ref[hh] = alpha * acc
        return m_new, l_new

    def steps(c, ml, st_cur, cmax_cur, p_prev, p_cur, key0=None):
        return tuple(step(hh, c, *ml[hh], st_cur, cmax_cur, p_prev, p_cur, key0) for hh in hs)

    for hh in hs:
        fill(hh, 0, st_a, cmax_a)
        fill(hh, 1, st_b, cmax_b)
    p_b[...] = jnp.zeros_like(p_b)
    acc_ref[...] = jnp.zeros_like(acc_ref)

    def pair(jj, ml):
        ml = steps(2 * jj, ml, st_a, cmax_a, p_b, p_a)
        return steps(2 * jj + 1, ml, st_b, cmax_b, p_a, p_b)

    ml = lax.fori_loop(
        0, i, pair, tuple((jnp.full((1, tq), NEG_BIG, F32), jnp.zeros((1, tq), F32)) for _ in hs))
    ml = steps(2 * i, ml, st_a, cmax_a, p_b, p_a, key0=0)
    ml = steps(2 * i + 1, ml, st_b, cmax_b, p_a, p_b, key0=tk)
    for hh in hs:
        acc = acc_ref[hh] + weighted_values(hh, 2 * i + 1, p_b[hh])
        o_ref[hh] = ((acc / ml[hh][1]).T * gate_ref[hh].astype(F32)).astype(o_ref.dtype)


def _fox_attention(q, k, v, gate, c, kaux):
    heads, s, _ = q.shape
    tq = _tile(s, 1024)
    hps = ATTN_HEADS_PER_STEP if heads % ATTN_HEADS_PER_STEP == 0 else 1
    kern = functools.partial(_fox_attn_kernel, tq=tq, heads_per_step=hps)
    once = pl.Buffered(1)
    kv_spec = lambda: pl.BlockSpec((hps, s, HEAD_DIM), lambda h, i: (h, 0, 0), pipeline_mode=once)
    return pl.pallas_call(
        kern,
        out_shape=jax.ShapeDtypeStruct((heads, s, HEAD_DIM), BF16),
        grid=(heads // hps, s // tq),
        in_specs=[pl.BlockSpec((hps, tq, HEAD_DIM), lambda h, i: (h, i, 0)),
                  kv_spec(), kv_spec(),
                  pl.BlockSpec((s, HEAD_DIM), lambda h, i: (0, 0), pipeline_mode=once),
                  pl.BlockSpec((8, HEAD_DIM), lambda h, i: (i * (tq // 8), 0)),
                  pl.BlockSpec((hps, tq, HEAD_DIM), lambda h, i: (h, i, 0))],
        out_specs=pl.BlockSpec((hps, tq, HEAD_DIM), lambda h, i: (h, i, 0)),
        scratch_shapes=[pltpu.VMEM((hps, tq // 2, tq), F32), pltpu.VMEM((hps, tq // 2, tq), F32),
                        pltpu.VMEM((hps, 1, tq), F32), pltpu.VMEM((hps, 1, tq), F32),
                        pltpu.VMEM((hps, tq // 2, tq), BF16), pltpu.VMEM((hps, tq // 2, tq), BF16),
                        pltpu.VMEM((hps, HEAD_DIM, tq), F32)],
        compiler_params=_params("parallel", "arbitrary"),
        name="fox_attention",
    )(q, k, v, kaux, c, gate)


def _hgrn_level_widths(t):
    ws = []
    w = t // 2
    while w >= HGRN_DIAG:
        ws.append(w)
        w //= 2
    return ws


def _hgrn_sum_masks(t):
    r = jnp.arange(t)[:, None]
    c = jnp.arange(t)[None, :]
    blocks = [c <= r]
    for w in _hgrn_level_widths(t):
        ref = (r // (2 * w)) * (2 * w) + w - 1
        q_side = (r % (2 * w)) >= w
        blocks.append(jnp.where(q_side, (c > ref) & (c <= r), (c > r) & (c <= ref)))
    return jnp.concatenate(blocks, axis=0).astype(BF16)


def _hgrn_scan_kernel(q_ref, k_ref, g_ref, v_ref, gate_ref, gain_ref, masks_ref, ones_ref,
                      o_ref, state_ref, *, t, heads_per_step):
    @pl.when(pl.program_id(1) == 0)
    def _():
        state_ref[...] = jnp.zeros_like(state_ref)

    for hh in range(heads_per_step):
        lanes = slice(hh * HEAD_DIM, (hh + 1) * HEAD_DIM)
        _hgrn_scan_head(q_ref.at[:, lanes], k_ref.at[:, lanes], g_ref.at[:, lanes], v_ref.at[:, lanes],
                        gate_ref.at[:, lanes], gain_ref, masks_ref, ones_ref, o_ref.at[:, lanes],
                        state_ref.at[hh], t)


def _hgrn_scan_head(q_ref, k_ref, g_ref, v_ref, gate_ref, gain_ref, masks_ref, ones_ref, o_ref,
                    state_ref, t):
    q = q_ref[...].astype(F32)
    k = k_ref[...].astype(F32)
    v_bf = v_ref[...]
    v = v_bf.astype(F32)
    g = g_ref[...]

    pieces = jnp.concatenate(_split_bf16(g, 2), axis=1)
    sums2 = jnp.dot(masks_ref[...], pieces, preferred_element_type=F32)
    sums = sums2[:, :HEAD_DIM] + sums2[:, HEAD_DIM:]
    b = sums[0:t]
    b_last = b[t - 1:t]

    st = state_ref[...]
    o = lax.dot_general((q * jnp.exp(b)).astype(BF16), st.astype(BF16), _NT, preferred_element_type=F32)

    rows = lax.broadcasted_iota(jnp.int32, (t, 1), 0)
    scores = jnp.zeros((t, t), F32)
    for lvl, w in enumerate(_hgrn_level_widths(t)):
        fac = jnp.exp(sums[(lvl + 1) * t:(lvl + 2) * t])
        q_side = (rows % (2 * w)) >= w
        a = jnp.where(q_side, q * fac, 0.0).astype(BF16)
        bk = jnp.where(q_side, 0.0, k * fac).astype(BF16)
        sc = lax.dot_general(a, bk, _NT, preferred_element_type=F32)
        if 2 * w < t:
            ri = lax.broadcasted_iota(jnp.int32, (t, t), 0) // (2 * w)
            cj = lax.broadcasted_iota(jnp.int32, (t, t), 1) // (2 * w)
            sc = jnp.where(ri == cj, sc, 0.0)
        scores = scores + sc
    o = o + jnp.dot(scores.astype(BF16), v_bf, preferred_element_type=F32)

    def group_roll(a, dd):
        return pltpu.roll(a.reshape(t // HGRN_DIAG, HGRN_DIAG, HEAD_DIM), dd, 1).reshape(t, HEAD_DIM)

    sub = rows % HGRN_DIAG
    rel = jnp.zeros_like(g)
    xs, vds = [q * k], [v]
    for dd in range(1, HGRN_DIAG):
        rel = rel + (group_roll(g, dd - 1) if dd > 1 else g)
        xs.append(q * group_roll(k, dd) * jnp.exp(jnp.where(sub >= dd, rel, NEG_BIG)))
        vds.append(group_roll(v, dd))
    ones2 = ones_ref[...]
    for dd in range(0, HGRN_DIAG, 2):
        pair = jnp.concatenate([xs[dd], xs[dd + 1]], axis=1).astype(BF16)
        rowsums = jnp.dot(pair, ones2, preferred_element_type=F32)
        o = o + rowsums[:, :HEAD_DIM] * vds[dd] + rowsums[:, HEAD_DIM:] * vds[dd + 1]

    kh = (k * jnp.exp(b_last - b)).astype(BF16)
    state_ref[...] = st * jnp.exp(b_last) + lax.dot_general(v_bf, kh, _TN, preferred_element_type=F32)

    ms = jnp.mean(o * o, axis=-1, keepdims=True)
    y = o * lax.rsqrt(ms + NORM_EPS) * gain_ref[...]
    o_ref[...] = (y * gate_ref[...].astype(F32)).astype(o_ref.dtype)


def _hgrn_scan(q, k, g, v, gate, out_gain):
    s, d = g.shape
    heads = d // HEAD_DIM
    t = _tile(s, HGRN_CHUNK)
    masks = _hgrn_sum_masks(t)
    nm = masks.shape[0]
    hps = HGRN_HEADS_PER_STEP if heads % HGRN_HEADS_PER_STEP == 0 else 1
    kern = functools.partial(_hgrn_scan_kernel, t=t, heads_per_step=hps)
    blk = lambda: pl.BlockSpec((t, hps * HEAD_DIM), lambda h, c: (c, h))
    return pl.pallas_call(
        kern,
        out_shape=jax.ShapeDtypeStruct((s, d), BF16),
        grid=(heads // hps, s // t),
        in_specs=[blk(), blk(), blk(), blk(), blk(),
                  pl.BlockSpec((1, HEAD_DIM), lambda h, c: (0, 0)),
                  pl.BlockSpec((nm, t), lambda h, c: (0, 0)),
                  pl.BlockSpec((2 * HEAD_DIM, 2 * HEAD_DIM), lambda h, c: (0, 0))],
        out_specs=blk(),
        scratch_shapes=[pltpu.VMEM((hps, HEAD_DIM, HEAD_DIM), F32)],
        compiler_params=_params("parallel", "arbitrary"),
        name="hgrn_scan",
    )(q, k, g, v, gate, out_gain.reshape(1, HEAD_DIM).astype(F32), masks,
      jnp.kron(jnp.eye(2, dtype=BF16), jnp.ones((HEAD_DIM, HEAD_DIM), BF16)))


def kernel(x, fox_w_in, fox_w_out, fox_q_gain, fox_k_gain, fox_fgate_bias, hgrn_w_in, hgrn_w_out,
           hgrn_out_gain, hgrn_lb_logits, mixer_norm_gain, mlp_norm_gain, mlp_w_up, mlp_w_down,
           final_norm_gain):
    batch, seq, d = x.shape
    depth = mixer_norm_gain.shape[0]
    fox_w_in_b, fox_w_out_b = fox_w_in.astype(BF16), fox_w_out.astype(BF16)
    hgrn_w_in_b, hgrn_w_out_b = hgrn_w_in.astype(BF16), hgrn_w_out.astype(BF16)
    mlp_w_up_b, mlp_w_down_b = mlp_w_up.astype(BF16), mlp_w_down.astype(BF16)
    q_scale = HEAD_DIM ** -0.5 * LOG2E
    outs = []
    for bi in range(batch):
        h = x[bi]
        for i in range(depth):
            j = i // 2
            n = _rmsnorm(h, mixer_norm_gain[i], BF16)
            if i % 2 == 0:
                q = _in_proj(n, fox_w_in_b, j, 0, "fox_in_proj_q", head_gain=fox_q_gain[j] * q_scale,
                             head_major=True)
                k = _in_proj(n, fox_w_in_b, j, d, "fox_in_proj_k", head_gain=fox_k_gain[j], head_major=True)
                v = _in_proj(n, fox_w_in_b, j, 2 * d, "fox_in_proj_v", head_major=True)
                gate = _in_proj(n, fox_w_in_b, j, 3 * d, "fox_in_proj_gate", act=_sigmoid, head_major=True)
                c, kaux = _fox_forget_cumsum(n, fox_w_in_b[j, :, 4 * d:], fox_fgate_bias[j])
                mix_in = _fox_attention(q, k, v, gate, c, kaux)
                h = _proj_residual(mix_in, fox_w_out_b, j, h, name="mixer_out_proj")
            else:
                q = _in_proj(n, hgrn_w_in_b, j, 0, "hgrn_in_proj_q", act=_silu)
                g, k = _hgrn_in_proj_forget(n, hgrn_w_in_b, j, hgrn_lb_logits, i)
                v = _in_proj(n, hgrn_w_in_b, j, 2 * d, "hgrn_in_proj_v")
                gate = _in_proj(n, hgrn_w_in_b, j, 3 * d, "hgrn_in_proj_gate", act=_silu)
                mix_in = _hgrn_scan(q, k, g, v, gate, hgrn_out_gain[j])
                h = _proj_residual(mix_in, hgrn_w_out_b, j, h, name="mixer_out_proj")
            n = _rmsnorm(h, mlp_norm_gain[i], BF16)
            a = _mlp_up(n, mlp_w_up_b, i)
            h = _proj_residual(a, mlp_w_down_b, i, h, name="mlp_down")
        outs.append(_rmsnorm(h, final_norm_gain, x.dtype))
    return jnp.stack(outs, axis=0)
```

```python
import functools

import jax
import jax.numpy as jnp
from jax import lax
from jax.experimental import pallas as pl
from jax.experimental.pallas import tpu as pltpu

F32 = jnp.float32
BF16 = jnp.bfloat16

HEAD_DIM = 128
NORM_EPS = 1e-6
NEG_BIG = -1e30
VMEM_LIMIT_BYTES = 56 * 1024 * 1024

LOG2E = 1.4426950408889634
AUX_PIECES = 3
AUX_CONST_LANE = 96

ATTN_HEADS_PER_STEP = 2
ATTN_PAIRS_PER_TRIP = 2

HGRN_CHUNK = 256
HGRN_HEADS_PER_STEP = 4
HGRN_DIAG = 8

_NT = (((1,), (1,)), ((), ()))
_TN = (((0,), (0,)), ((), ()))


def _tile(n, pref):
    t = min(n, pref)
    while n % t:
        t //= 2
    return t


def _params(*sem):
    return pltpu.CompilerParams(dimension_semantics=sem, vmem_limit_bytes=VMEM_LIMIT_BYTES)


def _split_bf16(a, terms):
    parts = []
    r = a
    for _ in range(terms):
        p = r.astype(BF16)
        parts.append(p)
        r = r - p.astype(F32)
    return parts


def _sigmoid(x):
    return 1.0 / (1.0 + jnp.exp(-x))


def _log_sigmoid(x):
    return jnp.minimum(x, 0.0) - jnp.log(1.0 + jnp.exp(-jnp.abs(x)))


def _rmsnorm_kernel(x_ref, g_ref, o_ref):
    x = x_ref[...]
    ms = jnp.mean(x * x, axis=-1, keepdims=True)
    o_ref[...] = (x * lax.rsqrt(ms + NORM_EPS) * g_ref[...]).astype(o_ref.dtype)


def _rmsnorm(x, gain, out_dtype):
    s, d = x.shape
    tm = _tile(s, 256)
    return pl.pallas_call(
        _rmsnorm_kernel,
        out_shape=jax.ShapeDtypeStruct((s, d), out_dtype),
        grid=(s // tm,),
        in_specs=[pl.BlockSpec((tm, d), lambda i: (i, 0)),
                  pl.BlockSpec((1, d), lambda i: (0, 0))],
        out_specs=pl.BlockSpec((tm, d), lambda i: (i, 0)),
        compiler_params=_params("parallel"),
        name="rmsnorm",
    )(x, gain.reshape(1, d).astype(F32))


def _matmul_kernel(x_ref, w_ref, *refs, nk, n_aux, epilogue):
    aux, outs = refs[:n_aux], refs[n_aux:]
    if nk == 1:
        if len(x_ref.shape) == 3:
            x = jnp.concatenate([x_ref[hd] for hd in range(x_ref.shape[0])], axis=1)
        else:
            x = x_ref[...]
        epilogue(jnp.dot(x, w_ref[...], preferred_element_type=F32), aux, outs)
        return
    (h_ref,), (o_ref,) = aux, outs

    @pl.when(pl.program_id(2) == 0)
    def _():
        o_ref[...] = h_ref[...]

    o_ref[...] += jnp.dot(x_ref[...], w_ref[...], preferred_element_type=F32)


def _matmul(x, w, layer, *, tm, tn, tk, n_cols, w_col_block, epilogue, aux, aux_specs, out_shapes, name,
            out_specs=None):
    if x.ndim == 3:
        heads, m, _ = x.shape
        kdim = heads * HEAD_DIM
        assert tk == kdim
        x_spec = pl.BlockSpec((heads, tm, HEAD_DIM), lambda i, j, k: (0, i, 0))
    else:
        m, kdim = x.shape
        x_spec = pl.BlockSpec((tm, tk), lambda i, j, k: (i, k))
    nk = kdim // tk
    if nk > 1:
        assert epilogue is None and len(aux) == 1 and len(out_shapes) == 1 and out_shapes[0].dtype == F32
    kern = functools.partial(_matmul_kernel, nk=nk, n_aux=len(aux), epilogue=epilogue)
    return pl.pallas_call(
        kern,
        out_shape=out_shapes,
        grid=(m // tm, n_cols // tn, nk),
        in_specs=[x_spec,
                  pl.BlockSpec((None, tk, tn), lambda i, j, k: (layer, k, w_col_block(j)))] + list(aux_specs),
        out_specs=out_specs or [pl.BlockSpec((tm, tn), lambda i, j, k: (i, j)) for _ in out_shapes],
        compiler_params=_params("parallel", "parallel", "arbitrary"),
        name=name,
    )(x, w, *aux)


def _silu(x):
    return x * _sigmoid(x)


def _in_proj(n, w, layer, col0, name, *, act=None, head_gain=None, head_major=False):
    s, d = n.shape
    tm, tn = _tile(s, 1024), _tile(d, 512)
    blk0 = col0 // tn
    hpt = tn // HEAD_DIM

    def epilogue(acc, aux, outs):
        (o_ref,) = outs
        for c in range(hpt):
            xc = acc[:, c * HEAD_DIM:(c + 1) * HEAD_DIM]
            if head_gain is not None:
                ms = jnp.mean(xc * xc, axis=-1, keepdims=True)
                xc = xc * lax.rsqrt(ms + NORM_EPS) * aux[0][...]
            elif act is not None:
                xc = act(xc)
            if head_major:
                o_ref[c] = xc.astype(o_ref.dtype)
            else:
                o_ref[:, c * HEAD_DIM:(c + 1) * HEAD_DIM] = xc.astype(o_ref.dtype)

    aux, aux_specs = (), ()
    if head_gain is not None:
        aux = (head_gain.astype(F32).reshape(1, HEAD_DIM),)
        aux_specs = (pl.BlockSpec((1, HEAD_DIM), lambda i, j, k: (0, 0)),)
    if head_major:
        out_shape = jax.ShapeDtypeStruct((d // HEAD_DIM, s, HEAD_DIM), BF16)
        out_specs = [pl.BlockSpec((hpt, tm, HEAD_DIM), lambda i, j, k: (j, i, 0))]
    else:
        out_shape, out_specs = jax.ShapeDtypeStruct((s, d), BF16), None
    (out,) = _matmul(n, w, layer, tm=tm, tn=tn, tk=d, n_cols=d, w_col_block=lambda j: j + blk0,
                     epilogue=epilogue, aux=aux, aux_specs=aux_specs,
                     out_shapes=[out_shape], out_specs=out_specs, name=name)
    return out


def _hgrn_in_proj_forget(n, w_in, w_layer, lb_logits, layer):
    s, d = n.shape
    depth = lb_logits.shape[0]
    tm, tn = _tile(s, 1024), _tile(d, 512)
    nq = d // tn

    def epilogue(acc, aux, outs):
        (lb_ref,) = aux
        g_ref, k_ref = outs
        z = lb_ref[...]
        e = jnp.exp(z - jnp.max(z, axis=0, keepdims=True))
        p = e / jnp.sum(e, axis=0, keepdims=True)
        lb = jnp.zeros((1, tn), F32)
        for r in range(1, layer + 1):
            lb = lb + p[r:r + 1, :]
        f = lb + (1.0 - lb) * _sigmoid(acc)
        g_ref[...] = jnp.log(f)
        k_ref[...] = (1.0 - f).astype(k_ref.dtype)

    g, k = _matmul(n, w_in, w_layer, tm=tm, tn=tn, tk=d, n_cols=d, w_col_block=lambda j: j + nq,
                   epilogue=epilogue, aux=(lb_logits.astype(F32),),
                   aux_specs=(pl.BlockSpec((depth, tn), lambda i, j, k: (0, j)),),
                   out_shapes=[jax.ShapeDtypeStruct((s, d), F32), jax.ShapeDtypeStruct((s, d), BF16)],
                   name="hgrn_in_proj_forget")
    return g, k


def _proj_residual(x, w, layer, h, *, name):
    s, kdim = (x.shape[1], x.shape[0] * HEAD_DIM) if x.ndim == 3 else x.shape
    d = w.shape[2]
    tm = _tile(s, 1024)
    if kdim <= 4096:
        tn, tk = _tile(d, 512), kdim

        def epilogue(acc, aux, outs):
            (h_ref,) = aux
            (o_ref,) = outs
            o_ref[...] = h_ref[...] + acc
    else:
        tn, tk, epilogue = _tile(d, 2048), 1024, None

    (out,) = _matmul(x, w, layer, tm=tm, tn=tn, tk=tk, n_cols=d, w_col_block=lambda j: j,
                     epilogue=epilogue, aux=(h,),
                     aux_specs=(pl.BlockSpec((tm, tn), lambda i, j, k: (i, j)),),
                     out_shapes=[jax.ShapeDtypeStruct((s, d), F32)], name=name)
    return out


def _mlp_up(n, w_up, layer):
    s, d = n.shape
    f = w_up.shape[2]
    tm, tn = _tile(s, 1024), _tile(f, 512)

    def epilogue(acc, aux, outs):
        (o_ref,) = outs
        r = jnp.maximum(acc, 0.0)
        o_ref[...] = (r * r).astype(o_ref.dtype)

    (out,) = _matmul(n, w_up, layer, tm=tm, tn=tn, tk=d, n_cols=f, w_col_block=lambda j: j,
                     epilogue=epilogue, aux=(), aux_specs=(),
                     out_shapes=[jax.ShapeDtypeStruct((s, f), BF16)], name="mlp_up")
    return out


def _fox_forget_kernel(x_ref, wf_ref, bias_ref, tri_ref, perm_ref, c_ref, kaux_ref, carry_ref):
    i = pl.program_id(0)

    @pl.when(i == 0)
    def _():
        carry_ref[...] = jnp.zeros_like(carry_ref)

    fz = jnp.dot(x_ref[...], wf_ref[...], preferred_element_type=F32)
    lf = _log_sigmoid(fz + bias_ref[...]) * LOG2E
    tri = tri_ref[...]
    cs = jnp.zeros_like(lf)
    for part in _split_bf16(lf, 3):
        cs = cs + jnp.dot(tri, part, preferred_element_type=F32)
    c = cs + carry_ref[...]
    c_ref[...] = c
    carry_ref[...] = c[c.shape[0] - 1:, :]

    pieces = jnp.concatenate(_split_bf16(c, AUX_PIECES), axis=1)
    lane = lax.broadcasted_iota(jnp.int32, (1, HEAD_DIM), 1)
    const = jnp.where(jnp.logical_and(lane >= AUX_CONST_LANE, lane < AUX_CONST_LANE + AUX_PIECES), 1.0, 0.0)
    kaux = jnp.dot(pieces, perm_ref[...], preferred_element_type=F32) + const
    kaux_ref[...] = kaux.astype(kaux_ref.dtype)


def _fox_forget_cumsum(n, w_f, bias):
    s, d = n.shape
    heads = w_f.shape[1]
    assert AUX_PIECES * heads <= AUX_CONST_LANE
    tm = _tile(s, 512)
    r = jnp.arange(tm)
    tri = (r[None, :] <= r[:, None]).astype(BF16)
    wf = jnp.zeros((d, HEAD_DIM), BF16).at[:, :heads].set(w_f)
    b = jnp.zeros((1, HEAD_DIM), F32).at[0, :heads].set(bias.astype(F32))
    hh = jnp.arange(heads)
    perm = jnp.zeros((AUX_PIECES * HEAD_DIM, HEAD_DIM), BF16)
    for p in range(AUX_PIECES):
        perm = perm.at[p * HEAD_DIM + hh, AUX_PIECES * hh + p].set(-1.0)
    return pl.pallas_call(
        _fox_forget_kernel,
        out_shape=[jax.ShapeDtypeStruct((s, HEAD_DIM), F32), jax.ShapeDtypeStruct((s, HEAD_DIM), BF16)],
        grid=(s // tm,),
        in_specs=[pl.BlockSpec((tm, d), lambda i: (i, 0)),
                  pl.BlockSpec((d, HEAD_DIM), lambda i: (0, 0)),
                  pl.BlockSpec((1, HEAD_DIM), lambda i: (0, 0)),
                  pl.BlockSpec((tm, tm), lambda i: (0, 0)),
                  pl.BlockSpec((AUX_PIECES * HEAD_DIM, HEAD_DIM), lambda i: (0, 0))],
        out_specs=[pl.BlockSpec((tm, HEAD_DIM), lambda i: (i, 0)),
                   pl.BlockSpec((tm, HEAD_DIM), lambda i: (i, 0))],
        scratch_shapes=[pltpu.VMEM((1, HEAD_DIM), F32)],
        compiler_params=_params("arbitrary"),
        name="fox_forget_cumsum",
    )(n, wf, b, tri, perm)


def _fox_attn_kernel(q_ref, k_ref, v_ref, kaux_ref, c_ref, gate_ref, o_ref,
                     st_a, st_b, cmax_a, cmax_b, p_a, p_b, acc_ref, *, tq, heads_per_step):
    i = pl.program_id(1)
    tk = tq // 2
    lane = lax.broadcasted_iota(jnp.int32, (1, HEAD_DIM), 1)
    hs = range(heads_per_step)

    q_aug = []
    for hh in hs:
        h = pl.program_id(0) * heads_per_step + hh
        c0 = jnp.sum(jnp.where(lane == h, c_ref[0:1, :], 0.0), axis=1, keepdims=True)
        qaux = jnp.where(jnp.logical_and(lane >= AUX_PIECES * h, lane < AUX_PIECES * (h + 1)), 1.0, 0.0)
        for p, piece in enumerate(_split_bf16(c0, AUX_PIECES)):
            qaux = jnp.where(lane == AUX_CONST_LANE + p, piece.astype(F32), qaux)
        q_aug.append(jnp.concatenate(
            [q_ref[hh], jnp.broadcast_to(qaux.astype(BF16), (tq, HEAD_DIM))], axis=1))

    def logits(hh, c):
        row0 = pl.multiple_of(c * tk, tk)
        k_aug = jnp.concatenate([k_ref[hh, pl.ds(row0, tk), :], kaux_ref[pl.ds(row0, tk), :]], axis=1)
        return lax.dot_general(k_aug, q_aug[hh], _NT, preferred_element_type=F32)

    def weighted_values(hh, c, p):
        row0 = pl.multiple_of(c * tk, tk)
        return lax.dot_general(v_ref[hh, pl.ds(row0, tk), :], p, _TN, preferred_element_type=F32)

    def fill(hh, c, st_buf, cmax_buf):
        st = logits(hh, c)
        st_buf[hh] = st
        cmax_buf[hh] = jnp.max(st, axis=0, keepdims=True)

    def step(hh, c, m, l, st_cur, cmax_cur, p_prev, p_cur, key0=None):
        acc = acc_ref[hh] + weighted_values(hh, jnp.maximum(c - 1, 0), p_prev[hh])
        st = st_cur[hh]
        if key0 is None:
            cmax = cmax_cur[hh]
            fill(hh, c + 2, st_cur, cmax_cur)
        else:
            keys = lax.broadcasted_iota(jnp.int32, (tk, tq), 0) + key0
            queries = lax.broadcasted_iota(jnp.int32, (tk, tq), 1)
            st = jnp.where(keys <= queries, st, NEG_BIG)
            cmax = jnp.max(st, axis=0, keepdims=True)
        m_new = jnp.maximum(m, cmax)
        alpha = jnp.exp2(m - m_new)
        p = jnp.exp2(st - m_new)
        l_new = alpha * l + jnp.sum(p, axis=0, keepdims=True)
        p_cur[hh] = p.astype(BF16)
        acc_ref[hh] = alpha * acc
        return m_new, l_new

    def steps(c, ml, st_cur, cmax_cur, p_prev, p_cur, key0=None):
        return tuple(step(hh, c, *ml[hh], st_cur, cmax_cur, p_prev, p_cur, key0) for hh in hs)

    for hh in hs:
        fill(hh, 0, st_a, cmax_a)
        fill(hh, 1, st_b, cmax_b)
    p_b[...] = jnp.zeros_like(p_b)
    acc_ref[...] = jnp.zeros_like(acc_ref)

    def pairs(c0, n_pairs, ml):
        for r in range(n_pairs):
            ml = steps(c0 + 2 * r, ml, st_a, cmax_a, p_b, p_a)
            ml = steps(c0 + 2 * r + 1, ml, st_b, cmax_b, p_a, p_b)
        return ml

    ml = tuple((jnp.full((1, tq), NEG_BIG, F32), jnp.zeros((1, tq), F32)) for _ in hs)
    n_trips = i // ATTN_PAIRS_PER_TRIP
    ml = lax.fori_loop(
        0, n_trips, lambda jj, c: pairs(2 * ATTN_PAIRS_PER_TRIP * jj, ATTN_PAIRS_PER_TRIP, c), ml)
    ml = lax.fori_loop(n_trips * ATTN_PAIRS_PER_TRIP, i, lambda jj, c: pairs(2 * jj, 1, c), ml)
    ml = steps(2 * i, ml, st_a, cmax_a, p_b, p_a, key0=0)
    ml = steps(2 * i + 1, ml, st_b, cmax_b, p_a, p_b, key0=tk)
    for hh in hs:
        acc = acc_ref[hh] + weighted_values(hh, 2 * i + 1, p_b[hh])
        o_ref[hh] = ((acc / ml[hh][1]).T * gate_ref[hh].astype(F32)).astype(o_ref.dtype)


def _fox_attention(q, k, v, gate, c, kaux):
    heads, s, _ = q.shape
    tq = _tile(s, 1024)
    hps = ATTN_HEADS_PER_STEP if heads % ATTN_HEADS_PER_STEP == 0 else 1
    kern = functools.partial(_fox_attn_kernel, tq=tq, heads_per_step=hps)
    once = pl.Buffered(1)
    kv_spec = lambda: pl.BlockSpec((hps, s, HEAD_DIM), lambda h, i: (h, 0, 0), pipeline_mode=once)
    return pl.pallas_call(
        kern,
        out_shape=jax.ShapeDtypeStruct((heads, s, HEAD_DIM), BF16),
        grid=(heads // hps, s // tq),
        in_specs=[pl.BlockSpec((hps, tq, HEAD_DIM), lambda h, i: (h, i, 0)),
                  kv_spec(), kv_spec(),
                  pl.BlockSpec((s, HEAD_DIM), lambda h, i: (0, 0), pipeline_mode=once),
                  pl.BlockSpec((8, HEAD_DIM), lambda h, i: (i * (tq // 8), 0)),
                  pl.BlockSpec((hps, tq, HEAD_DIM), lambda h, i: (h, i, 0))],
        out_specs=pl.BlockSpec((hps, tq, HEAD_DIM), lambda h, i: (h, i, 0)),
        scratch_shapes=[pltpu.VMEM((hps, tq // 2, tq), F32), pltpu.VMEM((hps, tq // 2, tq), F32),
                        pltpu.VMEM((hps, 1, tq), F32), pltpu.VMEM((hps, 1, tq), F32),
                        pltpu.VMEM((hps, tq // 2, tq), BF16), pltpu.VMEM((hps, tq // 2, tq), BF16),
                        pltpu.VMEM((hps, HEAD_DIM, tq), F32)],
        compiler_params=_params("parallel", "arbitrary"),
        name="fox_attention",
    )(q, k, v, kaux, c, gate)


def _hgrn_level_widths(t):
    ws = []
    w = t // 2
    while w >= HGRN_DIAG:
        ws.append(w)
        w //= 2
    return ws


def _hgrn_sum_masks(t):
    r = jnp.arange(t)[:, None]
    c = jnp.arange(t)[None, :]
    blocks = [c <= r]
    for w in _hgrn_level_widths(t):
        ref = (r // (2 * w)) * (2 * w) + w - 1
        q_side = (r % (2 * w)) >= w
        blocks.append(jnp.where(q_side, (c > ref) & (c <= r), (c > r) & (c <= ref)))
    return jnp.concatenate(blocks, axis=0).astype(BF16)


def _hgrn_scan_kernel(q_ref, k_ref, g_ref, v_ref, gate_ref, gain_ref, masks_ref, ones_ref,
                      o_ref, state_ref, *, t, heads_per_step):
    @pl.when(pl.program_id(1) == 0)
    def _():
        state_ref[...] = jnp.zeros_like(state_ref)

    for hh in range(heads_per_step):
        lanes = slice(hh * HEAD_DIM, (hh + 1) * HEAD_DIM)
        _hgrn_scan_head(q_ref.at[:, lanes], k_ref.at[:, lanes], g_ref.at[:, lanes], v_ref.at[:, lanes],
                        gate_ref.at[:, lanes], gain_ref, masks_ref, ones_ref, o_ref.at[:, lanes],
                        state_ref.at[hh], t)


def _hgrn_scan_head(q_ref, k_ref, g_ref, v_ref, gate_ref, gain_ref, masks_ref, ones_ref, o_ref,
                    state_ref, t):
    q = q_ref[...].astype(F32)
    k = k_ref[...].astype(F32)
    v_bf = v_ref[...]
    v = v_bf.astype(F32)
    g = g_ref[...]

    pieces = jnp.concatenate(_split_bf16(g, 2), axis=1)
    sums2 = jnp.dot(masks_ref[...], pieces, preferred_element_type=F32)
    sums = sums2[:, :HEAD_DIM] + sums2[:, HEAD_DIM:]
    b = sums[0:t]
    b_last = b[t - 1:t]

    st = state_ref[...]
    o = lax.dot_general((q * jnp.exp(b)).astype(BF16), st.astype(BF16), _NT, preferred_element_type=F32)

    rows = lax.broadcasted_iota(jnp.int32, (t, 1), 0)
    scores = jnp.zeros((t, t), F32)
    for lvl, w in enumerate(_hgrn_level_widths(t)):
        fac = jnp.exp(sums[(lvl + 1) * t:(lvl + 2) * t])
        q_side = (rows % (2 * w)) >= w
        a = jnp.where(q_side, q * fac, 0.0).astype(BF16)
        bk = jnp.where(q_side, 0.0, k * fac).astype(BF16)
        sc = lax.dot_general(a, bk, _NT, preferred_element_type=F32)
        if 2 * w < t:
            ri = lax.broadcasted_iota(jnp.int32, (t, t), 0) // (2 * w)
            cj = lax.broadcasted_iota(jnp.int32, (t, t), 1) // (2 * w)
            sc = jnp.where(ri == cj, sc, 0.0)
        scores = scores + sc
    o = o + jnp.dot(scores.astype(BF16), v_bf, preferred_element_type=F32)

    def group_roll(a, dd):
        return pltpu.roll(a.reshape(t // HGRN_DIAG, HGRN_DIAG, HEAD_DIM), dd, 1).reshape(t, HEAD_DIM)

    sub = rows % HGRN_DIAG
    rel = jnp.zeros_like(g)
    xs, vds = [q * k], [v]
    for dd in range(1, HGRN_DIAG):
        rel = rel + (group_roll(g, dd - 1) if dd > 1 else g)
        xs.append(q * group_roll(k, dd) * jnp.exp(jnp.where(sub >= dd, rel, NEG_BIG)))
        vds.append(group_roll(v, dd))
    ones2 = ones_ref[...]
    for dd in range(0, HGRN_DIAG, 2):
        pair = jnp.concatenate([xs[dd], xs[dd + 1]], axis=1).astype(BF16)
        rowsums = jnp.dot(pair, ones2, preferred_element_type=F32)
        o = o + rowsums[:, :HEAD_DIM] * vds[dd] + rowsums[:, HEAD_DIM:] * vds[dd + 1]

    kh = (k * jnp.exp(b_last - b)).astype(BF16)
    state_ref[...] = st * jnp.exp(b_last) + lax.dot_general(v_bf, kh, _TN, preferred_element_type=F32)

    ms = jnp.mean(o * o, axis=-1, keepdims=True)
    y = o * lax.rsqrt(ms + NORM_EPS) * gain_ref[...]
    o_ref[...] = (y * gate_ref[...].astype(F32)).astype(o_ref.dtype)


def _hgrn_scan(q, k, g, v, gate, out_gain):
    s, d = g.shape
    heads = d // HEAD_DIM
    t = _tile(s, HGRN_CHUNK)
    masks = _hgrn_sum_masks(t)
    nm = masks.shape[0]
    hps = HGRN_HEADS_PER_STEP if heads % HGRN_HEADS_PER_STEP == 0 else 1
    kern = functools.partial(_hgrn_scan_kernel, t=t, heads_per_step=hps)
    blk = lambda: pl.BlockSpec((t, hps * HEAD_DIM), lambda h, c: (c, h))
    return pl.pallas_call(
        kern,
        out_shape=jax.ShapeDtypeStruct((s, d), BF16),
        grid=(heads // hps, s // t),
        in_specs=[blk(), blk(), blk(), blk(), blk(),
                  pl.BlockSpec((1, HEAD_DIM), lambda h, c: (0, 0)),
                  pl.BlockSpec((nm, t), lambda h, c: (0, 0)),
                  pl.BlockSpec((2 * HEAD_DIM, 2 * HEAD_DIM), lambda h, c: (0, 0))],
        out_specs=blk(),
        scratch_shapes=[pltpu.VMEM((hps, HEAD_DIM, HEAD_DIM), F32)],
        compiler_params=_params("parallel", "arbitrary"),
        name="hgrn_scan",
    )(q, k, g, v, gate, out_gain.reshape(1, HEAD_DIM).astype(F32), masks,
      jnp.kron(jnp.eye(2, dtype=BF16), jnp.ones((HEAD_DIM, HEAD_DIM), BF16)))


def kernel(x, fox_w_in, fox_w_out, fox_q_gain, fox_k_gain, fox_fgate_bias, hgrn_w_in, hgrn_w_out,
           hgrn_out_gain, hgrn_lb_logits, mixer_norm_gain, mlp_norm_gain, mlp_w_up, mlp_w_down,
           final_norm_gain):
    batch, seq, d = x.shape
    depth = mixer_norm_gain.shape[0]
    fox_w_in_b, fox_w_out_b = fox_w_in[:, :, :4 * d].astype(BF16), fox_w_out.astype(BF16)
    fox_w_f_b = fox_w_in[:, :, 4 * d:].astype(BF16)
    hgrn_w_in_b, hgrn_w_out_b = hgrn_w_in.astype(BF16), hgrn_w_out.astype(BF16)
    mlp_w_up_b, mlp_w_down_b = mlp_w_up.astype(BF16), mlp_w_down.astype(BF16)
    q_scale = HEAD_DIM ** -0.5 * LOG2E
    outs = []
    for bi in range(batch):
        h = x[bi]
        for i in range(depth):
            j = i // 2
            n = _rmsnorm(h, mixer_norm_gain[i], BF16)
            if i % 2 == 0:
                q = _in_proj(n, fox_w_in_b, j, 0, "fox_in_proj_q", head_gain=fox_q_gain[j] * q_scale,
                             head_major=True)
                k = _in_proj(n, fox_w_in_b, j, d, "fox_in_proj_k", head_gain=fox_k_gain[j], head_major=True)
                v = _in_proj(n, fox_w_in_b, j, 2 * d, "fox_in_proj_v", head_major=True)
                gate = _in_proj(n, fox_w_in_b, j, 3 * d, "fox_in_proj_gate", act=_sigmoid, head_major=True)
                c, kaux = _fox_forget_cumsum(n, fox_w_f_b[j], fox_fgate_bias[j])
                mix_in = _fox_attention(q, k, v, gate, c, kaux)
                h = _proj_residual(mix_in, fox_w_out_b, j, h, name="mixer_out_proj")
            else:
                q = _in_proj(n, hgrn_w_in_b, j, 0, "hgrn_in_proj_q", act=_silu)
                g, k = _hgrn_in_proj_forget(n, hgrn_w_in_b, j, hgrn_lb_logits, i)
                v = _in_proj(n, hgrn_w_in_b, j, 2 * d, "hgrn_in_proj_v")
                gate = _in_proj(n, hgrn_w_in_b, j, 3 * d, "hgrn_in_proj_gate", act=_silu)
                mix_in = _hgrn_scan(q, k, g, v, gate, hgrn_out_gain[j])
                h = _proj_residual(mix_in, hgrn_w_out_b, j, h, name="mixer_out_proj")
            n = _rmsnorm(h, mlp_norm_gain[i], BF16)
            a = _mlp_up(n, mlp_w_up_b, i)
            h = _proj_residual(a, mlp_w_down_b, i, h, name="mlp_down")
        outs.append(_rmsnorm(h, final_norm_gain, x.dtype))
    return jnp.stack(outs, axis=0)
```

```python
import functools

import jax
import jax.numpy as jnp
from jax import lax
from jax.experimental import pallas as pl
from jax.experimental.pallas import tpu as pltpu

F32 = jnp.float32
BF16 = jnp.bfloat16

HEAD_DIM = 128
NORM_EPS = 1e-6
NEG_BIG = -1e30
VMEM_LIMIT_BYTES = 56 * 1024 * 1024

MATMUL_TN = 1024
LOG2E = 1.4426950408889634
AUX_PIECES = 3
AUX_CONST_LANE = 96

ATTN_HEADS_PER_STEP = 2
ATTN_PAIRS_PER_TRIP = 2

HGRN_CHUNK = 256
HGRN_HEADS_PER_STEP = 4
HGRN_DIAG = 8

_NT = (((1,), (1,)), ((), ()))
_TN = (((0,), (0,)), ((), ()))


def _tile(n, pref):
    t = min(n, pref)
    while n % t:
        t //= 2
    return t


def _params(*sem):
    return pltpu.CompilerParams(dimension_semantics=sem, vmem_limit_bytes=VMEM_LIMIT_BYTES)


def _split_bf16(a, terms):
    parts = []
    r = a
    for _ in range(terms):
        p = r.astype(BF16)
        parts.append(p)
        r = r - p.astype(F32)
    return parts


def _sigmoid(x):
    return 1.0 / (1.0 + jnp.exp(-x))


def _log_sigmoid(x):
    return jnp.minimum(x, 0.0) - jnp.log(1.0 + jnp.exp(-jnp.abs(x)))


def _rmsnorm_kernel(x_ref, g_ref, o_ref):
    x = x_ref[...]
    ms = jnp.mean(x * x, axis=-1, keepdims=True)
    o_ref[...] = (x * lax.rsqrt(ms + NORM_EPS) * g_ref[...]).astype(o_ref.dtype)


def _rmsnorm(x, gain, out_dtype):
    s, d = x.shape
    tm = _tile(s, 256)
    return pl.pallas_call(
        _rmsnorm_kernel,
        out_shape=jax.ShapeDtypeStruct((s, d), out_dtype),
        grid=(s // tm,),
        in_specs=[pl.BlockSpec((tm, d), lambda i: (i, 0)),
                  pl.BlockSpec((1, d), lambda i: (0, 0))],
        out_specs=pl.BlockSpec((tm, d), lambda i: (i, 0)),
        compiler_params=_params("parallel"),
        name="rmsnorm",
    )(x, gain.reshape(1, d).astype(F32))


def _matmul_kernel(x_ref, w_ref, *refs, nk, n_aux, epilogue):
    aux, outs = refs[:n_aux], refs[n_aux:]
    if nk == 1:
        if len(x_ref.shape) == 3:
            x = jnp.concatenate([x_ref[hd] for hd in range(x_ref.shape[0])], axis=1)
        else:
            x = x_ref[...]
        epilogue(jnp.dot(x, w_ref[...], preferred_element_type=F32), aux, outs)
        return
    (h_ref,), (o_ref,) = aux, outs

    @pl.when(pl.program_id(2) == 0)
    def _():
        o_ref[...] = h_ref[...]

    o_ref[...] += jnp.dot(x_ref[...], w_ref[...], preferred_element_type=F32)


def _matmul(x, w, layer, *, tm, tn, tk, n_cols, w_col_block, epilogue, aux, aux_specs, out_shapes, name,
            out_specs=None):
    if x.ndim == 3:
        heads, m, _ = x.shape
        kdim = heads * HEAD_DIM
        assert tk == kdim
        x_spec = pl.BlockSpec((heads, tm, HEAD_DIM), lambda i, j, k: (0, i, 0))
    else:
        m, kdim = x.shape
        x_spec = pl.BlockSpec((tm, tk), lambda i, j, k: (i, k))
    nk = kdim // tk
    if nk > 1:
        assert epilogue is None and len(aux) == 1 and len(out_shapes) == 1 and out_shapes[0].dtype == F32
    kern = functools.partial(_matmul_kernel, nk=nk, n_aux=len(aux), epilogue=epilogue)
    return pl.pallas_call(
        kern,
        out_shape=out_shapes,
        grid=(m // tm, n_cols // tn, nk),
        in_specs=[x_spec,
                  pl.BlockSpec((None, tk, tn), lambda i, j, k: (layer, k, w_col_block(j)))] + list(aux_specs),
        out_specs=out_specs or [pl.BlockSpec((tm, tn), lambda i, j, k: (i, j)) for _ in out_shapes],
        compiler_params=_params("parallel", "parallel", "arbitrary"),
        name=name,
    )(x, w, *aux)


def _silu(x):
    return x * _sigmoid(x)


def _in_proj(n, w, layer, col0, name, *, act=None, head_gain=None, head_major=False):
    s, d = n.shape
    tm, tn = _tile(s, 1024), _tile(d, MATMUL_TN)
    blk0 = col0 // tn
    hpt = tn // HEAD_DIM

    def epilogue(acc, aux, outs):
        (o_ref,) = outs
        for c in range(hpt):
            xc = acc[:, c * HEAD_DIM:(c + 1) * HEAD_DIM]
            if head_gain is not None:
                ms = jnp.mean(xc * xc, axis=-1, keepdims=True)
                xc = xc * lax.rsqrt(ms + NORM_EPS) * aux[0][...]
            elif act is not None:
                xc = act(xc)
            if head_major:
                o_ref[c] = xc.astype(o_ref.dtype)
            else:
                o_ref[:, c * HEAD_DIM:(c + 1) * HEAD_DIM] = xc.astype(o_ref.dtype)

    aux, aux_specs = (), ()
    if head_gain is not None:
        aux = (head_gain.astype(F32).reshape(1, HEAD_DIM),)
        aux_specs = (pl.BlockSpec((1, HEAD_DIM), lambda i, j, k: (0, 0)),)
    if head_major:
        out_shape = jax.ShapeDtypeStruct((d // HEAD_DIM, s, HEAD_DIM), BF16)
        out_specs = [pl.BlockSpec((hpt, tm, HEAD_DIM), lambda i, j, k: (j, i, 0))]
    else:
        out_shape, out_specs = jax.ShapeDtypeStruct((s, d), BF16), None
    (out,) = _matmul(n, w, layer, tm=tm, tn=tn, tk=d, n_cols=d, w_col_block=lambda j: j + blk0,
                     epilogue=epilogue, aux=aux, aux_specs=aux_specs,
                     out_shapes=[out_shape], out_specs=out_specs, name=name)
    return out


def _hgrn_in_proj_forget(n, w_in, w_layer, lb_logits, layer):
    s, d = n.shape
    depth = lb_logits.shape[0]
    tm, tn = _tile(s, 1024), _tile(d, MATMUL_TN)
    nq = d // tn

    def epilogue(acc, aux, outs):
        (lb_ref,) = aux
        g_ref, k_ref = outs
        z = lb_ref[...]
        e = jnp.exp(z - jnp.max(z, axis=0, keepdims=True))
        p = e / jnp.sum(e, axis=0, keepdims=True)
        lb = jnp.zeros((1, tn), F32)
        for r in range(1, layer + 1):
            lb = lb + p[r:r + 1, :]
        f = lb + (1.0 - lb) * _sigmoid(acc)
        g_ref[...] = jnp.log(f)
        k_ref[...] = (1.0 - f).astype(k_ref.dtype)

    g, k = _matmul(n, w_in, w_layer, tm=tm, tn=tn, tk=d, n_cols=d, w_col_block=lambda j: j + nq,
                   epilogue=epilogue, aux=(lb_logits.astype(F32),),
                   aux_specs=(pl.BlockSpec((depth, tn), lambda i, j, k: (0, j)),),
                   out_shapes=[jax.ShapeDtypeStruct((s, d), F32), jax.ShapeDtypeStruct((s, d), BF16)],
                   name="hgrn_in_proj_forget")
    return g, k


def _proj_residual(x, w, layer, h, *, name):
    s, kdim = (x.shape[1], x.shape[0] * HEAD_DIM) if x.ndim == 3 else x.shape
    d = w.shape[2]
    tm = _tile(s, 1024)
    if kdim <= 4096:
        tn, tk = _tile(d, MATMUL_TN), kdim

        def epilogue(acc, aux, outs):
            (h_ref,) = aux
            (o_ref,) = outs
            o_ref[...] = h_ref[...] + acc
    else:
        tn, tk, epilogue = _tile(d, MATMUL_TN), 4096, None

    (out,) = _matmul(x, w, layer, tm=tm, tn=tn, tk=tk, n_cols=d, w_col_block=lambda j: j,
                     epilogue=epilogue, aux=(h,),
                     aux_specs=(pl.BlockSpec((tm, tn), lambda i, j, k: (i, j)),),
                     out_shapes=[jax.ShapeDtypeStruct((s, d), F32)], name=name)
    return out


def _mlp_up(n, w_up, layer):
    s, d = n.shape
    f = w_up.shape[2]
    tm, tn = _tile(s, 1024), _tile(f, MATMUL_TN)

    def epilogue(acc, aux, outs):
        (o_ref,) = outs
        r = jnp.maximum(acc, 0.0)
        o_ref[...] = (r * r).astype(o_ref.dtype)

    (out,) = _matmul(n, w_up, layer, tm=tm, tn=tn, tk=d, n_cols=f, w_col_block=lambda j: j,
                     epilogue=epilogue, aux=(), aux_specs=(),
                     out_shapes=[jax.ShapeDtypeStruct((s, f), BF16)], name="mlp_up")
    return out


def _fox_forget_kernel(x_ref, wf_ref, bias_ref, tri_ref, perm_ref, c_ref, kaux_ref, carry_ref):
    i = pl.program_id(0)

    @pl.when(i == 0)
    def _():
        carry_ref[...] = jnp.zeros_like(carry_ref)

    fz = jnp.dot(x_ref[...], wf_ref[...], preferred_element_type=F32)
    lf = _log_sigmoid(fz + bias_ref[...]) * LOG2E
    tri = tri_ref[...]
    cs = jnp.zeros_like(lf)
    for part in _split_bf16(lf, 3):
        cs = cs + jnp.dot(tri, part, preferred_element_type=F32)
    c = cs + carry_ref[...]
    c_ref[...] = c
    carry_ref[...] = c[c.shape[0] - 1:, :]

    pieces = jnp.concatenate(_split_bf16(c, AUX_PIECES), axis=1)
    lane = lax.broadcasted_iota(jnp.int32, (1, HEAD_DIM), 1)
    const = jnp.where(jnp.logical_and(lane >= AUX_CONST_LANE, lane < AUX_CONST_LANE + AUX_PIECES), 1.0, 0.0)
    kaux = jnp.dot(pieces, perm_ref[...], preferred_element_type=F32) + const
    kaux_ref[...] = kaux.astype(kaux_ref.dtype)


def _fox_forget_cumsum(n, w_f, bias):
    s, d = n.shape
    heads = w_f.shape[1]
    assert AUX_PIECES * heads <= AUX_CONST_LANE
    tm = _tile(s, 512)
    r = jnp.arange(tm)
    tri = (r[None, :] <= r[:, None]).astype(BF16)
    wf = jnp.zeros((d, HEAD_DIM), BF16).at[:, :heads].set(w_f)
    b = jnp.zeros((1, HEAD_DIM), F32).at[0, :heads].set(bias.astype(F32))
    hh = jnp.arange(heads)
    perm = jnp.zeros((AUX_PIECES * HEAD_DIM, HEAD_DIM), BF16)
    for p in range(AUX_PIECES):
        perm = perm.at[p * HEAD_DIM + hh, AUX_PIECES * hh + p].set(-1.0)
    return pl.pallas_call(
        _fox_forget_kernel,
        out_shape=[jax.ShapeDtypeStruct((s, HEAD_DIM), F32), jax.ShapeDtypeStruct((s, HEAD_DIM), BF16)],
        grid=(s // tm,),
        in_specs=[pl.BlockSpec((tm, d), lambda i: (i, 0)),
                  pl.BlockSpec((d, HEAD_DIM), lambda i: (0, 0)),
                  pl.BlockSpec((1, HEAD_DIM), lambda i: (0, 0)),
                  pl.BlockSpec((tm, tm), lambda i: (0, 0)),
                  pl.BlockSpec((AUX_PIECES * HEAD_DIM, HEAD_DIM), lambda i: (0, 0))],
        out_specs=[pl.BlockSpec((tm, HEAD_DIM), lambda i: (i, 0)),
                   pl.BlockSpec((tm, HEAD_DIM), lambda i: (i, 0))],
        scratch_shapes=[pltpu.VMEM((1, HEAD_DIM), F32)],
        compiler_params=_params("arbitrary"),
        name="fox_forget_cumsum",
    )(n, wf, b, tri, perm)


def _fox_attn_kernel(q_ref, k_ref, v_ref, kaux_ref, c_ref, gate_ref, o_ref,
                     st_a, st_b, cmax_a, cmax_b, p_a, p_b, acc_ref, *, tq, heads_per_step):
    i = pl.program_id(1)
    tk = tq // 2
    lane = lax.broadcasted_iota(jnp.int32, (1, HEAD_DIM), 1)
    hs = range(heads_per_step)

    q_aug = []
    for hh in hs:
        h = pl.program_id(0) * heads_per_step + hh
        c0 = jnp.sum(jnp.where(lane == h, c_ref[0:1, :], 0.0), axis=1, keepdims=True)
        qaux = jnp.where(jnp.logical_and(lane >= AUX_PIECES * h, lane < AUX_PIECES * (h + 1)), 1.0, 0.0)
        for p, piece in enumerate(_split_bf16(c0, AUX_PIECES)):
            qaux = jnp.where(lane == AUX_CONST_LANE + p, piece.astype(F32), qaux)
        q_aug.append(jnp.concatenate(
            [q_ref[hh], jnp.broadcast_to(qaux.astype(BF16), (tq, HEAD_DIM))], axis=1))

    def logits(hh, c):
        row0 = pl.multiple_of(c * tk, tk)
        k_aug = jnp.concatenate([k_ref[hh, pl.ds(row0, tk), :], kaux_ref[pl.ds(row0, tk), :]], axis=1)
        return lax.dot_general(k_aug, q_aug[hh], _NT, preferred_element_type=F32)

    def weighted_values(hh, c, p):
        row0 = pl.multiple_of(c * tk, tk)
        return lax.dot_general(v_ref[hh, pl.ds(row0, tk), :], p, _TN, preferred_element_type=F32)

    def fill(hh, c, st_buf, cmax_buf):
        st = logits(hh, c)
        st_buf[hh] = st
        cmax_buf[hh] = jnp.max(st, axis=0, keepdims=True)

    def step(hh, c, m, l, st_cur, cmax_cur, p_prev, p_cur, key0=None):
        acc = acc_ref[hh] + weighted_values(hh, jnp.maximum(c - 1, 0), p_prev[hh])
        st = st_cur[hh]
        if key0 is None:
            cmax = cmax_cur[hh]
            fill(hh, c + 2, st_cur, cmax_cur)
        else:
            keys = lax.broadcasted_iota(jnp.int32, (tk, tq), 0) + key0
            queries = lax.broadcasted_iota(jnp.int32, (tk, tq), 1)
            st = jnp.where(keys <= queries, st, NEG_BIG)
            cmax = jnp.max(st, axis=0, keepdims=True)
        m_new = jnp.maximum(m, cmax)
        alpha = jnp.exp2(m - m_new)
        p = jnp.exp2(st - m_new)
        l_new = alpha * l + jnp.sum(p, axis=0, keepdims=True)
        p_cur[hh] = p.astype(BF16)
        acc_ref[hh] = alpha * acc
        return m_new, l_new

    def steps(c, ml, st_cur, cmax_cur, p_prev, p_cur, key0=None):
        return tuple(step(hh, c, *ml[hh], st_cur, cmax_cur, p_prev, p_cur, key0) for hh in hs)

    for hh in hs:
        fill(hh, 0, st_a, cmax_a)
        fill(hh, 1, st_b, cmax_b)
    p_b[...] = jnp.zeros_like(p_b)
    acc_ref[...] = jnp.zeros_like(acc_ref)

    def pairs(c0, n_pairs, ml):
        for r in range(n_pairs):
            ml = steps(c0 + 2 * r, ml, st_a, cmax_a, p_b, p_a)
            ml = steps(c0 + 2 * r + 1, ml, st_b, cmax_b, p_a, p_b)
        return ml

    ml = tuple((jnp.full((1, tq), NEG_BIG, F32), jnp.zeros((1, tq), F32)) for _ in hs)
    n_trips = i // ATTN_PAIRS_PER_TRIP
    ml = lax.fori_loop(
        0, n_trips, lambda jj, c: pairs(2 * ATTN_PAIRS_PER_TRIP * jj, ATTN_PAIRS_PER_TRIP, c), ml)
    ml = lax.fori_loop(n_trips * ATTN_PAIRS_PER_TRIP, i, lambda jj, c: pairs(2 * jj, 1, c), ml)
    ml = steps(2 * i, ml, st_a, cmax_a, p_b, p_a, key0=0)
    ml = steps(2 * i + 1, ml, st_b, cmax_b, p_a, p_b, key0=tk)
    for hh in hs:
        acc = acc_ref[hh] + weighted_values(hh, 2 * i + 1, p_b[hh])
        o_ref[hh] = ((acc / ml[hh][1]).T * gate_ref[hh].astype(F32)).astype(o_ref.dtype)


def _fox_attention(q, k, v, gate, c, kaux):
    heads, s, _ = q.shape
    tq = _tile(s, 1024)
    hps = ATTN_HEADS_PER_STEP if heads % ATTN_HEADS_PER_STEP == 0 else 1
    kern = functools.partial(_fox_attn_kernel, tq=tq, heads_per_step=hps)
    once = pl.Buffered(1)
    kv_spec = lambda: pl.BlockSpec((hps, s, HEAD_DIM), lambda h, i: (h, 0, 0), pipeline_mode=once)
    return pl.pallas_call(
        kern,
        out_shape=jax.ShapeDtypeStruct((heads, s, HEAD_DIM), BF16),
        grid=(heads // hps, s // tq),
        in_specs=[pl.BlockSpec((hps, tq, HEAD_DIM), lambda h, i: (h, i, 0)),
                  kv_spec(), kv_spec(),
                  pl.BlockSpec((s, HEAD_DIM), lambda h, i: (0, 0), pipeline_mode=once),
                  pl.BlockSpec((8, HEAD_DIM), lambda h, i: (i * (tq // 8), 0)),
                  pl.BlockSpec((hps, tq, HEAD_DIM), lambda h, i: (h, i, 0))],
        out_specs=pl.BlockSpec((hps, tq, HEAD_DIM), lambda h, i: (h, i, 0)),
        scratch_shapes=[pltpu.VMEM((hps, tq // 2, tq), F32), pltpu.VMEM((hps, tq // 2, tq), F32),
                        pltpu.VMEM((hps, 1, tq), F32), pltpu.VMEM((hps, 1, tq), F32),
                        pltpu.VMEM((hps, tq // 2, tq), BF16), pltpu.VMEM((hps, tq // 2, tq), BF16),
                        pltpu.VMEM((hps, HEAD_DIM, tq), F32)],
        compiler_params=_params("parallel", "arbitrary"),
        name="fox_attention",
    )(q, k, v, kaux, c, gate)


def _hgrn_level_widths(t):
    ws = []
    w = t // 2
    while w >= HGRN_DIAG:
        ws.append(w)
        w //= 2
    return ws


def _hgrn_sum_masks(t):
    r = jnp.arange(t)[:, None]
    c = jnp.arange(t)[None, :]
    blocks = [c <= r]
    for w in _hgrn_level_widths(t):
        ref = (r // (2 * w)) * (2 * w) + w - 1
        q_side = (r % (2 * w)) >= w
        blocks.append(jnp.where(q_side, (c > ref) & (c <= r), (c > r) & (c <= ref)))
    return jnp.concatenate(blocks, axis=0).astype(BF16)


def _hgrn_scan_kernel(q_ref, k_ref, g_ref, v_ref, gate_ref, gain_ref, masks_ref, ones_ref,
                      o_ref, state_ref, *, t, heads_per_step):
    @pl.when(pl.program_id(1) == 0)
    def _():
        state_ref[...] = jnp.zeros_like(state_ref)

    for hh in range(heads_per_step):
        lanes = slice(hh * HEAD_DIM, (hh + 1) * HEAD_DIM)
        _hgrn_scan_head(q_ref.at[:, lanes], k_ref.at[:, lanes], g_ref.at[:, lanes], v_ref.at[:, lanes],
                        gate_ref.at[:, lanes], gain_ref, masks_ref, ones_ref, o_ref.at[:, lanes],
                        state_ref.at[hh], t)


def _hgrn_scan_head(q_ref, k_ref, g_ref, v_ref, gate_ref, gain_ref, masks_ref, ones_ref, o_ref,
                    state_ref, t):
    q = q_ref[...].astype(F32)
    k = k_ref[...].astype(F32)
    v_bf = v_ref[...]
    v = v_bf.astype(F32)
    g = g_ref[...]

    pieces = jnp.concatenate(_split_bf16(g, 2), axis=1)
    sums2 = jnp.dot(masks_ref[...], pieces, preferred_element_type=F32)
    sums = sums2[:, :HEAD_DIM] + sums2[:, HEAD_DIM:]
    b = sums[0:t]
    b_last = b[t - 1:t]

    st = state_ref[...]
    o = lax.dot_general((q * jnp.exp(b)).astype(BF16), st.astype(BF16), _NT, preferred_element_type=F32)

    rows = lax.broadcasted_iota(jnp.int32, (t, 1), 0)
    scores = jnp.zeros((t, t), F32)
    for lvl, w in enumerate(_hgrn_level_widths(t)):
        fac = jnp.exp(sums[(lvl + 1) * t:(lvl + 2) * t])
        q_side = (rows % (2 * w)) >= w
        a = jnp.where(q_side, q * fac, 0.0).astype(BF16)
        bk = jnp.where(q_side, 0.0, k * fac).astype(BF16)
        sc = lax.dot_general(a, bk, _NT, preferred_element_type=F32)
        if 2 * w < t:
            ri = lax.broadcasted_iota(jnp.int32, (t, t), 0) // (2 * w)
            cj = lax.broadcasted_iota(jnp.int32, (t, t), 1) // (2 * w)
            sc = jnp.where(ri == cj, sc, 0.0)
        scores = scores + sc
    o = o + jnp.dot(scores.astype(BF16), v_bf, preferred_element_type=F32)

    def group_roll(a, dd):
        return pltpu.roll(a.reshape(t // HGRN_DIAG, HGRN_DIAG, HEAD_DIM), dd, 1).reshape(t, HEAD_DIM)

    sub = rows % HGRN_DIAG
    rel = jnp.zeros_like(g)
    xs, vds = [q * k], [v]
    for dd in range(1, HGRN_DIAG):
        rel = rel + (group_roll(g, dd - 1) if dd > 1 else g)
        xs.append(q * group_roll(k, dd) * jnp.exp(jnp.where(sub >= dd, rel, NEG_BIG)))
        vds.append(group_roll(v, dd))
    ones2 = ones_ref[...]
    for dd in range(0, HGRN_DIAG, 2):
        pair = jnp.concatenate([xs[dd], xs[dd + 1]], axis=1).astype(BF16)
        rowsums = jnp.dot(pair, ones2, preferred_element_type=F32)
        o = o + rowsums[:, :HEAD_DIM] * vds[dd] + rowsums[:, HEAD_DIM:] * vds[dd + 1]

    kh = (k * jnp.exp(b_last - b)).astype(BF16)
    state_ref[...] = st * jnp.exp(b_last) + lax.dot_general(v_bf, kh, _TN, preferred_element_type=F32)

    ms = jnp.mean(o * o, axis=-1, keepdims=True)
    y = o * lax.rsqrt(ms + NORM_EPS) * gain_ref[...]
    o_ref[...] = (y * gate_ref[...].astype(F32)).astype(o_ref.dtype)


def _hgrn_scan(q, k, g, v, gate, out_gain):
    s, d = g.shape
    heads = d // HEAD_DIM
    t = _tile(s, HGRN_CHUNK)
    masks = _hgrn_sum_masks(t)
    nm = masks.shape[0]
    hps = HGRN_HEADS_PER_STEP if heads % HGRN_HEADS_PER_STEP == 0 else 1
    kern = functools.partial(_hgrn_scan_kernel, t=t, heads_per_step=hps)
    blk = lambda: pl.BlockSpec((t, hps * HEAD_DIM), lambda h, c: (c, h))
    return pl.pallas_call(
        kern,
        out_shape=jax.ShapeDtypeStruct((s, d), BF16),
        grid=(heads // hps, s // t),
        in_specs=[blk(), blk(), blk(), blk(), blk(),
                  pl.BlockSpec((1, HEAD_DIM), lambda h, c: (0, 0)),
                  pl.BlockSpec((nm, t), lambda h, c: (0, 0)),
                  pl.BlockSpec((2 * HEAD_DIM, 2 * HEAD_DIM), lambda h, c: (0, 0))],
        out_specs=blk(),
        scratch_shapes=[pltpu.VMEM((hps, HEAD_DIM, HEAD_DIM), F32)],
        compiler_params=_params("parallel", "arbitrary"),
        name="hgrn_scan",
    )(q, k, g, v, gate, out_gain.reshape(1, HEAD_DIM).astype(F32), masks,
      jnp.kron(jnp.eye(2, dtype=BF16), jnp.ones((HEAD_DIM, HEAD_DIM), BF16)))


def kernel(x, fox_w_in, fox_w_out, fox_q_gain, fox_k_gain, fox_fgate_bias, hgrn_w_in, hgrn_w_out,
           hgrn_out_gain, hgrn_lb_logits, mixer_norm_gain, mlp_norm_gain, mlp_w_up, mlp_w_down,
           final_norm_gain):
    batch, seq, d = x.shape
    depth = mixer_norm_gain.shape[0]
    fox_w_in_b, fox_w_out_b = fox_w_in[:, :, :4 * d].astype(BF16), fox_w_out.astype(BF16)
    fox_w_f_b = fox_w_in[:, :, 4 * d:].astype(BF16)
    hgrn_w_in_b, hgrn_w_out_b = hgrn_w_in.astype(BF16), hgrn_w_out.astype(BF16)
    mlp_w_up_b, mlp_w_down_b = mlp_w_up.astype(BF16), mlp_w_down.astype(BF16)
    q_scale = HEAD_DIM ** -0.5 * LOG2E
    outs = []
    for bi in range(batch):
        h = x[bi]
        for i in range(depth):
            j = i // 2
            n = _rmsnorm(h, mixer_norm_gain[i], BF16)
            if i % 2 == 0:
                q = _in_proj(n, fox_w_in_b, j, 0, "fox_in_proj_q", head_gain=fox_q_gain[j] * q_scale,
                             head_major=True)
                k = _in_proj(n, fox_w_in_b, j, d, "fox_in_proj_k", head_gain=fox_k_gain[j], head_major=True)
                v = _in_proj(n, fox_w_in_b, j, 2 * d, "fox_in_proj_v", head_major=True)
                gate = _in_proj(n, fox_w_in_b, j, 3 * d, "fox_in_proj_gate", act=_sigmoid, head_major=True)
                c, kaux = _fox_forget_cumsum(n, fox_w_f_b[j], fox_fgate_bias[j])
                mix_in = _fox_attention(q, k, v, gate, c, kaux)
                h = _proj_residual(mix_in, fox_w_out_b, j, h, name="mixer_out_proj")
            else:
                q = _in_proj(n, hgrn_w_in_b, j, 0, "hgrn_in_proj_q", act=_silu)
                g, k = _hgrn_in_proj_forget(n, hgrn_w_in_b, j, hgrn_lb_logits, i)
                v = _in_proj(n, hgrn_w_in_b, j, 2 * d, "hgrn_in_proj_v")
                gate = _in_proj(n, hgrn_w_in_b, j, 3 * d, "hgrn_in_proj_gate", act=_silu)
                mix_in = _hgrn_scan(q, k, g, v, gate, hgrn_out_gain[j])
                h = _proj_residual(mix_in, hgrn_w_out_b, j, h, name="mixer_out_proj")
            n = _rmsnorm(h, mlp_norm_gain[i], BF16)
            a = _mlp_up(n, mlp_w_up_b, i)
            h = _proj_residual(a, mlp_w_down_b, i, h, name="mlp_down")
        outs.append(_rmsnorm(h, final_norm_gain, x.dtype))
    return jnp.stack(outs, axis=0)
```

```python
import functools

import jax
import jax.numpy as jnp
from jax import lax
from jax.experimental import pallas as pl
from jax.experimental.pallas import tpu as pltpu

F32 = jnp.float32
BF16 = jnp.bfloat16

HEAD_DIM = 128
NORM_EPS = 1e-6
NEG_BIG = -1e30
VMEM_LIMIT_BYTES = 56 * 1024 * 1024

MATMUL_TN = 1024
LOG2E = 1.4426950408889634
AUX_PIECES = 3
AUX_CONST_LANE = 96

ATTN_TQ = 1024
ATTN_HEADS_PER_STEP = 2
ATTN_PAIRS_PER_TRIP = 4

HGRN_CHUNK = 256
HGRN_HEADS_PER_STEP = 4
HGRN_DIAG = 8

_NT = (((1,), (1,)), ((), ()))
_TN = (((0,), (0,)), ((), ()))


def _tile(n, pref):
    t = min(n, pref)
    while n % t:
        t //= 2
    return t


def _params(*sem):
    return pltpu.CompilerParams(dimension_semantics=sem, vmem_limit_bytes=VMEM_LIMIT_BYTES)


def _split_bf16(a, terms):
    parts = []
    r = a
    for _ in range(terms):
        p = r.astype(BF16)
        parts.append(p)
        r = r - p.astype(F32)
    return parts


def _sigmoid(x):
    return 1.0 / (1.0 + jnp.exp(-x))


def _log_sigmoid(x):
    return jnp.minimum(x, 0.0) - jnp.log(1.0 + jnp.exp(-jnp.abs(x)))


def _rmsnorm_kernel(x_ref, g_ref, o_ref):
    x = x_ref[...]
    ms = jnp.mean(x * x, axis=-1, keepdims=True)
    o_ref[...] = (x * lax.rsqrt(ms + NORM_EPS) * g_ref[...]).astype(o_ref.dtype)


def _rmsnorm(x, gain, out_dtype):
    s, d = x.shape
    tm = _tile(s, 256)
    return pl.pallas_call(
        _rmsnorm_kernel,
        out_shape=jax.ShapeDtypeStruct((s, d), out_dtype),
        grid=(s // tm,),
        in_specs=[pl.BlockSpec((tm, d), lambda i: (i, 0)),
                  pl.BlockSpec((1, d), lambda i: (0, 0))],
        out_specs=pl.BlockSpec((tm, d), lambda i: (i, 0)),
        compiler_params=_params("parallel"),
        name="rmsnorm",
    )(x, gain.reshape(1, d).astype(F32))


def _matmul_kernel(x_ref, w_ref, *refs, nk, n_aux, epilogue):
    aux, outs = refs[:n_aux], refs[n_aux:]
    if nk == 1:
        if len(x_ref.shape) == 3:
            x = jnp.concatenate([x_ref[hd] for hd in range(x_ref.shape[0])], axis=1)
        else:
            x = x_ref[...]
        epilogue(jnp.dot(x, w_ref[...], preferred_element_type=F32), aux, outs)
        return
    (h_ref,), (o_ref,) = aux, outs

    @pl.when(pl.program_id(2) == 0)
    def _():
        o_ref[...] = h_ref[...]

    o_ref[...] += jnp.dot(x_ref[...], w_ref[...], preferred_element_type=F32)


def _matmul(x, w, layer, *, tm, tn, tk, n_cols, w_col_block, epilogue, aux, aux_specs, out_shapes, name,
            out_specs=None):
    if x.ndim == 3:
        heads, m, _ = x.shape
        kdim = heads * HEAD_DIM
        assert tk == kdim
        x_spec = pl.BlockSpec((heads, tm, HEAD_DIM), lambda i, j, k: (0, i, 0))
    else:
        m, kdim = x.shape
        x_spec = pl.BlockSpec((tm, tk), lambda i, j, k: (i, k))
    nk = kdim // tk
    if nk > 1:
        assert epilogue is None and len(aux) == 1 and len(out_shapes) == 1 and out_shapes[0].dtype == F32
    kern = functools.partial(_matmul_kernel, nk=nk, n_aux=len(aux), epilogue=epilogue)
    return pl.pallas_call(
        kern,
        out_shape=out_shapes,
        grid=(m // tm, n_cols // tn, nk),
        in_specs=[x_spec,
                  pl.BlockSpec((None, tk, tn), lambda i, j, k: (layer, k, w_col_block(j)))] + list(aux_specs),
        out_specs=out_specs or [pl.BlockSpec((tm, tn), lambda i, j, k: (i, j)) for _ in out_shapes],
        compiler_params=_params("parallel", "parallel", "arbitrary"),
        name=name,
    )(x, w, *aux)


def _silu(x):
    return x * _sigmoid(x)


def _in_proj(n, w, layer, col0, name, *, act=None, head_gain=None, head_major=False, transposed_chunk=None):
    s, d = n.shape
    tm, tn = _tile(s, 1024), _tile(d, MATMUL_TN)
    blk0 = col0 // tn
    hpt = tn // HEAD_DIM

    def epilogue(acc, aux, outs):
        (o_ref,) = outs
        for c in range(hpt):
            xc = acc[:, c * HEAD_DIM:(c + 1) * HEAD_DIM]
            if head_gain is not None:
                ms = jnp.mean(xc * xc, axis=-1, keepdims=True)
                xc = xc * lax.rsqrt(ms + NORM_EPS) * aux[0][...]
            elif act is not None:
                xc = act(xc)
            if transposed_chunk:
                for r in range(tm // transposed_chunk):
                    rows = slice(r * transposed_chunk, (r + 1) * transposed_chunk)
                    o_ref[c, r] = xc[rows, :].T.astype(o_ref.dtype)
            elif head_major:
                o_ref[c] = xc.astype(o_ref.dtype)
            else:
                o_ref[:, c * HEAD_DIM:(c + 1) * HEAD_DIM] = xc.astype(o_ref.dtype)

    aux, aux_specs = (), ()
    if head_gain is not None:
        aux = (head_gain.astype(F32).reshape(1, HEAD_DIM),)
        aux_specs = (pl.BlockSpec((1, HEAD_DIM), lambda i, j, k: (0, 0)),)
    if transposed_chunk:
        tc = transposed_chunk
        out_shape = jax.ShapeDtypeStruct((d // HEAD_DIM, s // tc, HEAD_DIM, tc), BF16)
        out_specs = [pl.BlockSpec((hpt, tm // tc, HEAD_DIM, tc), lambda i, j, k: (j, i, 0, 0))]
    elif head_major:
        out_shape = jax.ShapeDtypeStruct((d // HEAD_DIM, s, HEAD_DIM), BF16)
        out_specs = [pl.BlockSpec((hpt, tm, HEAD_DIM), lambda i, j, k: (j, i, 0))]
    else:
        out_shape, out_specs = jax.ShapeDtypeStruct((s, d), BF16), None
    (out,) = _matmul(n, w, layer, tm=tm, tn=tn, tk=d, n_cols=d, w_col_block=lambda j: j + blk0,
                     epilogue=epilogue, aux=aux, aux_specs=aux_specs,
                     out_shapes=[out_shape], out_specs=out_specs, name=name)
    return out


def _hgrn_in_proj_forget(n, w_in, w_layer, lb_logits, layer):
    s, d = n.shape
    depth = lb_logits.shape[0]
    tm, tn = _tile(s, 1024), _tile(d, MATMUL_TN)
    nq = d // tn

    def epilogue(acc, aux, outs):
        (lb_ref,) = aux
        g_ref, k_ref = outs
        z = lb_ref[...]
        e = jnp.exp(z - jnp.max(z, axis=0, keepdims=True))
        p = e / jnp.sum(e, axis=0, keepdims=True)
        lb = jnp.zeros((1, tn), F32)
        for r in range(1, layer + 1):
            lb = lb + p[r:r + 1, :]
        f = lb + (1.0 - lb) * _sigmoid(acc)
        g_ref[...] = jnp.log(f)
        k_ref[...] = (1.0 - f).astype(k_ref.dtype)

    g, k = _matmul(n, w_in, w_layer, tm=tm, tn=tn, tk=d, n_cols=d, w_col_block=lambda j: j + nq,
                   epilogue=epilogue, aux=(lb_logits.astype(F32),),
                   aux_specs=(pl.BlockSpec((depth, tn), lambda i, j, k: (0, j)),),
                   out_shapes=[jax.ShapeDtypeStruct((s, d), F32), jax.ShapeDtypeStruct((s, d), BF16)],
                   name="hgrn_in_proj_forget")
    return g, k


def _proj_residual(x, w, layer, h, *, name):
    s, kdim = (x.shape[1], x.shape[0] * HEAD_DIM) if x.ndim == 3 else x.shape
    d = w.shape[2]
    tm = _tile(s, 1024)
    if kdim <= 4096:
        tn, tk = _tile(d, MATMUL_TN), kdim

        def epilogue(acc, aux, outs):
            (h_ref,) = aux
            (o_ref,) = outs
            o_ref[...] = h_ref[...] + acc
    else:
        tn, tk, epilogue = _tile(d, MATMUL_TN), 4096, None

    (out,) = _matmul(x, w, layer, tm=tm, tn=tn, tk=tk, n_cols=d, w_col_block=lambda j: j,
                     epilogue=epilogue, aux=(h,),
                     aux_specs=(pl.BlockSpec((tm, tn), lambda i, j, k: (i, j)),),
                     out_shapes=[jax.ShapeDtypeStruct((s, d), F32)], name=name)
    return out


def _mlp_up(n, w_up, layer):
    s, d = n.shape
    f = w_up.shape[2]
    tm, tn = _tile(s, 1024), _tile(f, MATMUL_TN)

    def epilogue(acc, aux, outs):
        (o_ref,) = outs
        r = jnp.maximum(acc, 0.0)
        o_ref[...] = (r * r).astype(o_ref.dtype)

    (out,) = _matmul(n, w_up, layer, tm=tm, tn=tn, tk=d, n_cols=f, w_col_block=lambda j: j,
                     epilogue=epilogue, aux=(), aux_specs=(),
                     out_shapes=[jax.ShapeDtypeStruct((s, f), BF16)], name="mlp_up")
    return out


def _fox_forget_kernel(x_ref, wf_ref, bias_ref, tri_ref, perm_ref, c_ref, kaux_ref, carry_ref):
    i = pl.program_id(0)

    @pl.when(i == 0)
    def _():
        carry_ref[...] = jnp.zeros_like(carry_ref)

    fz = jnp.dot(x_ref[...], wf_ref[...], preferred_element_type=F32)
    lf = _log_sigmoid(fz + bias_ref[...]) * LOG2E
    tri = tri_ref[...]
    cs = jnp.zeros_like(lf)
    for part in _split_bf16(lf, 3):
        cs = cs + jnp.dot(tri, part, preferred_element_type=F32)
    c = cs + carry_ref[...]
    c_ref[...] = c
    carry_ref[...] = c[c.shape[0] - 1:, :]

    pieces = jnp.concatenate(_split_bf16(c, AUX_PIECES), axis=1)
    lane = lax.broadcasted_iota(jnp.int32, (1, HEAD_DIM), 1)
    const = jnp.where(jnp.logical_and(lane >= AUX_CONST_LANE, lane < AUX_CONST_LANE + AUX_PIECES), 1.0, 0.0)
    kaux = jnp.dot(pieces, perm_ref[...], preferred_element_type=F32) + const
    kaux_ref[...] = kaux.astype(kaux_ref.dtype)


def _fox_forget_cumsum(n, w_f, bias):
    s, d = n.shape
    heads = w_f.shape[1]
    assert AUX_PIECES * heads <= AUX_CONST_LANE
    tm = _tile(s, 512)
    r = jnp.arange(tm)
    tri = (r[None, :] <= r[:, None]).astype(BF16)
    wf = jnp.zeros((d, HEAD_DIM), BF16).at[:, :heads].set(w_f)
    b = jnp.zeros((1, HEAD_DIM), F32).at[0, :heads].set(bias.astype(F32))
    hh = jnp.arange(heads)
    perm = jnp.zeros((AUX_PIECES * HEAD_DIM, HEAD_DIM), BF16)
    for p in range(AUX_PIECES):
        perm = perm.at[p * HEAD_DIM + hh, AUX_PIECES * hh + p].set(-1.0)
    return pl.pallas_call(
        _fox_forget_kernel,
        out_shape=[jax.ShapeDtypeStruct((s, HEAD_DIM), F32), jax.ShapeDtypeStruct((s, HEAD_DIM), BF16)],
        grid=(s // tm,),
        in_specs=[pl.BlockSpec((tm, d), lambda i: (i, 0)),
                  pl.BlockSpec((d, HEAD_DIM), lambda i: (0, 0)),
                  pl.BlockSpec((1, HEAD_DIM), lambda i: (0, 0)),
                  pl.BlockSpec((tm, tm), lambda i: (0, 0)),
                  pl.BlockSpec((AUX_PIECES * HEAD_DIM, HEAD_DIM), lambda i: (0, 0))],
        out_specs=[pl.BlockSpec((tm, HEAD_DIM), lambda i: (i, 0)),
                   pl.BlockSpec((tm, HEAD_DIM), lambda i: (i, 0))],
        scratch_shapes=[pltpu.VMEM((1, HEAD_DIM), F32)],
        compiler_params=_params("arbitrary"),
        name="fox_forget_cumsum",
    )(n, wf, b, tri, perm)


def _fox_attn_kernel(q_ref, k_ref, v_ref, kaux_ref, c_ref, gate_ref, o_ref,
                     st_a, st_b, cmax_a, cmax_b, p_a, p_b, acc_ref, *, tq, heads_per_step):
    i = pl.program_id(1)
    tk = tq // 2
    lane = lax.broadcasted_iota(jnp.int32, (1, HEAD_DIM), 1)
    hs = range(heads_per_step)

    q_aug = []
    for hh in hs:
        h = pl.program_id(0) * heads_per_step + hh
        c0 = jnp.sum(jnp.where(lane == h, c_ref[0:1, :], 0.0), axis=1, keepdims=True)
        qaux = jnp.where(jnp.logical_and(lane >= AUX_PIECES * h, lane < AUX_PIECES * (h + 1)), 1.0, 0.0)
        for p, piece in enumerate(_split_bf16(c0, AUX_PIECES)):
            qaux = jnp.where(lane == AUX_CONST_LANE + p, piece.astype(F32), qaux)
        q_aug.append(jnp.concatenate(
            [q_ref[hh], jnp.broadcast_to(qaux.astype(BF16), (tq, HEAD_DIM))], axis=1))

    def logits(hh, c):
        row0 = pl.multiple_of(c * tk, tk)
        k_aug = jnp.concatenate([k_ref[hh, pl.ds(row0, tk), :], kaux_ref[pl.ds(row0, tk), :]], axis=1)
        return lax.dot_general(k_aug, q_aug[hh], _NT, preferred_element_type=F32)

    def weighted_values(hh, c, p):
        return jnp.dot(v_ref[hh, c], p, preferred_element_type=F32)

    def fill(hh, c, st_buf, cmax_buf):
        st = logits(hh, c)
        st_buf[hh] = st
        cmax_buf[hh] = jnp.max(st, axis=0, keepdims=True)

    def step(hh, c, m, l, st_cur, cmax_cur, p_prev, p_cur, key0=None):
        acc = acc_ref[hh] + weighted_values(hh, jnp.maximum(c - 1, 0), p_prev[hh])
        st = st_cur[hh]
        if key0 is None:
            cmax = cmax_cur[hh]
            fill(hh, c + 2, st_cur, cmax_cur)
        else:
            keys = lax.broadcasted_iota(jnp.int32, (tk, tq), 0) + key0
            queries = lax.broadcasted_iota(jnp.int32, (tk, tq), 1)
            st = jnp.where(keys <= queries, st, NEG_BIG)
            cmax = jnp.max(st, axis=0, keepdims=True)
        m_new = jnp.maximum(m, cmax)
        alpha = jnp.exp2(m - m_new)
        p = jnp.exp2(st - m_new)
        l_new = alpha * l + jnp.sum(p, axis=0, keepdims=True)
        p_cur[hh] = p.astype(BF16)
        acc_ref[hh] = alpha * acc
        return m_new, l_new

    def steps(c, ml, st_cur, cmax_cur, p_prev, p_cur, key0=None):
        return tuple(step(hh, c, *ml[hh], st_cur, cmax_cur, p_prev, p_cur, key0) for hh in hs)

    for hh in hs:
        fill(hh, 0, st_a, cmax_a)
        fill(hh, 1, st_b, cmax_b)
    p_b[...] = jnp.zeros_like(p_b)
    acc_ref[...] = jnp.zeros_like(acc_ref)

    def pairs(c0, n_pairs, ml):
        for r in range(n_pairs):
            ml = steps(c0 + 2 * r, ml, st_a, cmax_a, p_b, p_a)
            ml = steps(c0 + 2 * r + 1, ml, st_b, cmax_b, p_a, p_b)
        return ml

    ml = tuple((jnp.full((1, tq), NEG_BIG, F32), jnp.zeros((1, tq), F32)) for _ in hs)
    n_trips = i // ATTN_PAIRS_PER_TRIP
    ml = lax.fori_loop(
        0, n_trips, lambda jj, c: pairs(2 * ATTN_PAIRS_PER_TRIP * jj, ATTN_PAIRS_PER_TRIP, c), ml)
    ml = lax.fori_loop(n_trips * ATTN_PAIRS_PER_TRIP, i, lambda jj, c: pairs(2 * jj, 1, c), ml)
    ml = steps(2 * i, ml, st_a, cmax_a, p_b, p_a, key0=0)
    ml = steps(2 * i + 1, ml, st_b, cmax_b, p_a, p_b, key0=tk)
    for hh in hs:
        acc = acc_ref[hh] + weighted_values(hh, 2 * i + 1, p_b[hh])
        o_ref[hh] = ((acc / ml[hh][1]).T * gate_ref[hh].astype(F32)).astype(o_ref.dtype)


def _fox_attention(q, k, v, gate, c, kaux):
    heads, s, _ = q.shape
    tq = _tile(s, ATTN_TQ)
    hps = ATTN_HEADS_PER_STEP if heads % ATTN_HEADS_PER_STEP == 0 else 1
    kern = functools.partial(_fox_attn_kernel, tq=tq, heads_per_step=hps)
    once = pl.Buffered(1)
    kv_spec = lambda: pl.BlockSpec((hps, s, HEAD_DIM), lambda h, i: (h, 0, 0), pipeline_mode=once)
    return pl.pallas_call(
        kern,
        out_shape=jax.ShapeDtypeStruct((heads, s, HEAD_DIM), BF16),
        grid=(heads // hps, s // tq),
        in_specs=[pl.BlockSpec((hps, tq, HEAD_DIM), lambda h, i: (h, i, 0)),
                  kv_spec(),
                  pl.BlockSpec((hps, s // (tq // 2), HEAD_DIM, tq // 2), lambda h, i: (h, 0, 0, 0),
                               pipeline_mode=once),
                  pl.BlockSpec((s, HEAD_DIM), lambda h, i: (0, 0), pipeline_mode=once),
                  pl.BlockSpec((8, HEAD_DIM), lambda h, i: (i * (tq // 8), 0)),
                  pl.BlockSpec((hps, tq, HEAD_DIM), lambda h, i: (h, i, 0))],
        out_specs=pl.BlockSpec((hps, tq, HEAD_DIM), lambda h, i: (h, i, 0)),
        scratch_shapes=[pltpu.VMEM((hps, tq // 2, tq), F32), pltpu.VMEM((hps, tq // 2, tq), F32),
                        pltpu.VMEM((hps, 1, tq), F32), pltpu.VMEM((hps, 1, tq), F32),
                        pltpu.VMEM((hps, tq // 2, tq), BF16), pltpu.VMEM((hps, tq // 2, tq), BF16),
                        pltpu.VMEM((hps, HEAD_DIM, tq), F32)],
        compiler_params=_params("parallel", "arbitrary"),
        name="fox_attention",
    )(q, k, v, kaux, c, gate)


def _hgrn_level_widths(t):
    ws = []
    w = t // 2
    while w >= HGRN_DIAG:
        ws.append(w)
        w //= 2
    return ws


def _hgrn_sum_masks(t):
    r = jnp.arange(t)[:, None]
    c = jnp.arange(t)[None, :]
    blocks = [c <= r]
    for w in _hgrn_level_widths(t):
        ref = (r // (2 * w)) * (2 * w) + w - 1
        q_side = (r % (2 * w)) >= w
        blocks.append(jnp.where(q_side, (c > ref) & (c <= r), (c > r) & (c <= ref)))
    return jnp.concatenate(blocks, axis=0).astype(BF16)


def _hgrn_scan_kernel(q_ref, k_ref, g_ref, v_ref, gate_ref, gain_ref, masks_ref, ones_ref,
                      o_ref, state_ref, *, t, heads_per_step):
    @pl.when(pl.program_id(1) == 0)
    def _():
        state_ref[...] = jnp.zeros_like(state_ref)

    for hh in range(heads_per_step):
        lanes = slice(hh * HEAD_DIM, (hh + 1) * HEAD_DIM)
        _hgrn_scan_head(q_ref.at[:, lanes], k_ref.at[:, lanes], g_ref.at[:, lanes], v_ref.at[:, lanes],
                        gate_ref.at[:, lanes], gain_ref, masks_ref, ones_ref, o_ref.at[:, lanes],
                        state_ref.at[hh], t)


def _hgrn_scan_head(q_ref, k_ref, g_ref, v_ref, gate_ref, gain_ref, masks_ref, ones_ref, o_ref,
                    state_ref, t):
    q = q_ref[...].astype(F32)
    k = k_ref[...].astype(F32)
    v_bf = v_ref[...]
    v = v_bf.astype(F32)
    g = g_ref[...]

    pieces = jnp.concatenate(_split_bf16(g, 2), axis=1)
    sums2 = jnp.dot(masks_ref[...], pieces, preferred_element_type=F32)
    sums = sums2[:, :HEAD_DIM] + sums2[:, HEAD_DIM:]
    b = sums[0:t]
    b_last = b[t - 1:t]

    st = state_ref[...]
    o = lax.dot_general((q * jnp.exp(b)).astype(BF16), st.astype(BF16), _NT, preferred_element_type=F32)

    rows = lax.broadcasted_iota(jnp.int32, (t, 1), 0)
    scores = jnp.zeros((t, t), F32)
    for lvl, w in enumerate(_hgrn_level_widths(t)):
        fac = jnp.exp(sums[(lvl + 1) * t:(lvl + 2) * t])
        q_side = (rows % (2 * w)) >= w
        a = jnp.where(q_side, q * fac, 0.0).astype(BF16)
        bk = jnp.where(q_side, 0.0, k * fac).astype(BF16)
        sc = lax.dot_general(a, bk, _NT, preferred_element_type=F32)
        if 2 * w < t:
            ri = lax.broadcasted_iota(jnp.int32, (t, t), 0) // (2 * w)
            cj = lax.broadcasted_iota(jnp.int32, (t, t), 1) // (2 * w)
            sc = jnp.where(ri == cj, sc, 0.0)
        scores = scores + sc
    o = o + jnp.dot(scores.astype(BF16), v_bf, preferred_element_type=F32)

    def group_roll(a, dd):
        return pltpu.roll(a.reshape(t // HGRN_DIAG, HGRN_DIAG, HEAD_DIM), dd, 1).reshape(t, HEAD_DIM)

    sub = rows % HGRN_DIAG
    rel = jnp.zeros_like(g)
    xs, vds = [q * k], [v]
    for dd in range(1, HGRN_DIAG):
        rel = rel + (group_roll(g, dd - 1) if dd > 1 else g)
        xs.append(q * group_roll(k, dd) * jnp.exp(jnp.where(sub >= dd, rel, NEG_BIG)))
        vds.append(group_roll(v, dd))
    ones2 = ones_ref[...]
    for dd in range(0, HGRN_DIAG, 2):
        pair = jnp.concatenate([xs[dd], xs[dd + 1]], axis=1).astype(BF16)
        rowsums = jnp.dot(pair, ones2, preferred_element_type=F32)
        o = o + rowsums[:, :HEAD_DIM] * vds[dd] + rowsums[:, HEAD_DIM:] * vds[dd + 1]

    kh = (k * jnp.exp(b_last - b)).astype(BF16)
    state_ref[...] = st * jnp.exp(b_last) + lax.dot_general(v_bf, kh, _TN, preferred_element_type=F32)

    ms = jnp.mean(o * o, axis=-1, keepdims=True)
    y = o * lax.rsqrt(ms + NORM_EPS) * gain_ref[...]
    o_ref[...] = (y * gate_ref[...].astype(F32)).astype(o_ref.dtype)


def _hgrn_scan(q, k, g, v, gate, out_gain):
    s, d = g.shape
    heads = d // HEAD_DIM
    t = _tile(s, HGRN_CHUNK)
    masks = _hgrn_sum_masks(t)
    nm = masks.shape[0]
    hps = HGRN_HEADS_PER_STEP if heads % HGRN_HEADS_PER_STEP == 0 else 1
    kern = functools.partial(_hgrn_scan_kernel, t=t, heads_per_step=hps)
    blk = lambda: pl.BlockSpec((t, hps * HEAD_DIM), lambda h, c: (c, h))
    return pl.pallas_call(
        kern,
        out_shape=jax.ShapeDtypeStruct((s, d), BF16),
        grid=(heads // hps, s // t),
        in_specs=[blk(), blk(), blk(), blk(), blk(),
                  pl.BlockSpec((1, HEAD_DIM), lambda h, c: (0, 0)),
                  pl.BlockSpec((nm, t), lambda h, c: (0, 0)),
                  pl.BlockSpec((2 * HEAD_DIM, 2 * HEAD_DIM), lambda h, c: (0, 0))],
        out_specs=blk(),
        scratch_shapes=[pltpu.VMEM((hps, HEAD_DIM, HEAD_DIM), F32)],
        compiler_params=_params("parallel", "arbitrary"),
        name="hgrn_scan",
    )(q, k, g, v, gate, out_gain.reshape(1, HEAD_DIM).astype(F32), masks,
      jnp.kron(jnp.eye(2, dtype=BF16), jnp.ones((HEAD_DIM, HEAD_DIM), BF16)))


def kernel(x, fox_w_in, fox_w_out, fox_q_gain, fox_k_gain, fox_fgate_bias, hgrn_w_in, hgrn_w_out,
           hgrn_out_gain, hgrn_lb_logits, mixer_norm_gain, mlp_norm_gain, mlp_w_up, mlp_w_down,
           final_norm_gain):
    batch, seq, d = x.shape
    depth = mixer_norm_gain.shape[0]
    fox_w_in_b, fox_w_out_b = fox_w_in[:, :, :4 * d].astype(BF16), fox_w_out.astype(BF16)
    fox_w_f_b = fox_w_in[:, :, 4 * d:].astype(BF16)
    hgrn_w_in_b, hgrn_w_out_b = hgrn_w_in.astype(BF16), hgrn_w_out.astype(BF16)
    mlp_w_up_b, mlp_w_down_b = mlp_w_up.astype(BF16), mlp_w_down.astype(BF16)
    q_scale = HEAD_DIM ** -0.5 * LOG2E
    outs = []
    for bi in range(batch):
        h = x[bi]
        for i in range(depth):
            j = i // 2
            n = _rmsnorm(h, mixer_norm_gain[i], BF16)
            if i % 2 == 0:
                q = _in_proj(n, fox_w_in_b, j, 0, "fox_in_proj_q", head_gain=fox_q_gain[j] * q_scale,
                             head_major=True)
                k = _in_proj(n, fox_w_in_b, j, d, "fox_in_proj_k", head_gain=fox_k_gain[j], head_major=True)
                v = _in_proj(n, fox_w_in_b, j, 2 * d, "fox_in_proj_v",
                             transposed_chunk=_tile(seq, ATTN_TQ) // 2)
                gate = _in_proj(n, fox_w_in_b, j, 3 * d, "fox_in_proj_gate", act=_sigmoid, head_major=True)
                c, kaux = _fox_forget_cumsum(n, fox_w_f_b[j], fox_fgate_bias[j])
                mix_in = _fox_attention(q, k, v, gate, c, kaux)
                h = _proj_residual(mix_in, fox_w_out_b, j, h, name="mixer_out_proj")
            else:
                q = _in_proj(n, hgrn_w_in_b, j, 0, "hgrn_in_proj_q", act=_silu)
                g, k = _hgrn_in_proj_forget(n, hgrn_w_in_b, j, hgrn_lb_logits, i)
                v = _in_proj(n, hgrn_w_in_b, j, 2 * d, "hgrn_in_proj_v")
                gate = _in_proj(n, hgrn_w_in_b, j, 3 * d, "hgrn_in_proj_gate", act=_silu)
                mix_in = _hgrn_scan(q, k, g, v, gate, hgrn_out_gain[j])
                h = _proj_residual(mix_in, hgrn_w_out_b, j, h, name="mixer_out_proj")
            n = _rmsnorm(h, mlp_norm_gain[i], BF16)
            a = _mlp_up(n, mlp_w_up_b, i)
            h = _proj_residual(a, mlp_w_down_b, i, h, name="mlp_down")
        outs.append(_rmsnorm(h, final_norm_gain, x.dtype))
    return jnp.stack(outs, axis=0)
```

```python
import functools

import jax
import jax.numpy as jnp
from jax import lax
from jax.experimental import pallas as pl
from jax.experimental.pallas import tpu as pltpu

F32 = jnp.float32
BF16 = jnp.bfloat16

HEAD_DIM = 128
NORM_EPS = 1e-6
NEG_BIG = -1e30
VMEM_LIMIT_BYTES = 56 * 1024 * 1024

MATMUL_TN = 1024
LOG2E = 1.4426950408889634
AUX_PIECES = 3
AUX_CONST_LANE = 96

ATTN_TQ = 1024
ATTN_HEADS_PER_STEP = 2
ATTN_PAIRS_PER_TRIP = 4

HGRN_CHUNK = 256
HGRN_HEADS_PER_STEP = 4
HGRN_DIAG = 8

_NT = (((1,), (1,)), ((), ()))
_TN = (((0,), (0,)), ((), ()))


def _tile(n, pref):
    t = min(n, pref)
    while n % t:
        t //= 2
    return t


def _params(*sem):
    return pltpu.CompilerParams(dimension_semantics=sem, vmem_limit_bytes=VMEM_LIMIT_BYTES)


def _split_bf16(a, terms):
    parts = []
    r = a
    for _ in range(terms):
        p = r.astype(BF16)
        parts.append(p)
        r = r - p.astype(F32)
    return parts


def _sigmoid(x):
    return 1.0 / (1.0 + jnp.exp(-x))


def _log_sigmoid(x):
    return jnp.minimum(x, 0.0) - jnp.log(1.0 + jnp.exp(-jnp.abs(x)))


def _rmsnorm_kernel(x_ref, g_ref, o_ref):
    x = x_ref[...]
    ms = jnp.mean(x * x, axis=-1, keepdims=True)
    o_ref[...] = (x * lax.rsqrt(ms + NORM_EPS) * g_ref[...]).astype(o_ref.dtype)


def _rmsnorm(x, gain, out_dtype):
    s, d = x.shape
    tm = _tile(s, 256)
    return pl.pallas_call(
        _rmsnorm_kernel,
        out_shape=jax.ShapeDtypeStruct((s, d), out_dtype),
        grid=(s // tm,),
        in_specs=[pl.BlockSpec((tm, d), lambda i: (i, 0)),
                  pl.BlockSpec((1, d), lambda i: (0, 0))],
        out_specs=pl.BlockSpec((tm, d), lambda i: (i, 0)),
        compiler_params=_params("parallel"),
        name="rmsnorm",
    )(x, gain.reshape(1, d).astype(F32))


def _matmul_kernel(x_ref, w_ref, *refs, nk, n_aux, epilogue):
    aux, outs = refs[:n_aux], refs[n_aux:]
    if nk == 1:
        if len(x_ref.shape) == 3:
            x = jnp.concatenate([x_ref[hd] for hd in range(x_ref.shape[0])], axis=1)
        else:
            x = x_ref[...]
        epilogue(jnp.dot(x, w_ref[...], preferred_element_type=F32), aux, outs)
        return
    (h_ref,), (o_ref,) = aux, outs

    @pl.when(pl.program_id(2) == 0)
    def _():
        o_ref[...] = h_ref[...]

    o_ref[...] += jnp.dot(x_ref[...], w_ref[...], preferred_element_type=F32)


def _matmul(x, w, layer, *, tm, tn, tk, n_cols, w_col_block, epilogue, aux, aux_specs, out_shapes, name,
            out_specs=None):
    if x.ndim == 3:
        heads, m, _ = x.shape
        kdim = heads * HEAD_DIM
        assert tk == kdim
        x_spec = pl.BlockSpec((heads, tm, HEAD_DIM), lambda i, j, k: (0, i, 0))
    else:
        m, kdim = x.shape
        x_spec = pl.BlockSpec((tm, tk), lambda i, j, k: (i, k))
    nk = kdim // tk
    if nk > 1:
        assert epilogue is None and len(aux) == 1 and len(out_shapes) == 1 and out_shapes[0].dtype == F32
    kern = functools.partial(_matmul_kernel, nk=nk, n_aux=len(aux), epilogue=epilogue)
    return pl.pallas_call(
        kern,
        out_shape=out_shapes,
        grid=(m // tm, n_cols // tn, nk),
        in_specs=[x_spec,
                  pl.BlockSpec((None, tk, tn), lambda i, j, k: (layer, k, w_col_block(j)))] + list(aux_specs),
        out_specs=out_specs or [pl.BlockSpec((tm, tn), lambda i, j, k: (i, j)) for _ in out_shapes],
        compiler_params=_params("parallel", "parallel", "arbitrary"),
        name=name,
    )(x, w, *aux)


def _silu(x):
    return x * _sigmoid(x)


def _in_proj(n, w, layer, col0, name, *, act=None, head_gain=None, head_major=False, transposed_chunk=None):
    s, d = n.shape
    tm, tn = _tile(s, 1024), _tile(d, MATMUL_TN)
    blk0 = col0 // tn
    hpt = tn // HEAD_DIM

    def epilogue(acc, aux, outs):
        (o_ref,) = outs
        for c in range(hpt):
            xc = acc[:, c * HEAD_DIM:(c + 1) * HEAD_DIM]
            if head_gain is not None:
                ms = jnp.mean(xc * xc, axis=-1, keepdims=True)
                xc = xc * lax.rsqrt(ms + NORM_EPS) * aux[0][...]
            elif act is not None:
                xc = act(xc)
            if transposed_chunk:
                for r in range(tm // transposed_chunk):
                    rows = slice(r * transposed_chunk, (r + 1) * transposed_chunk)
                    o_ref[c, r] = xc[rows, :].T.astype(o_ref.dtype)
            elif head_major:
                o_ref[c] = xc.astype(o_ref.dtype)
            else:
                o_ref[:, c * HEAD_DIM:(c + 1) * HEAD_DIM] = xc.astype(o_ref.dtype)

    aux, aux_specs = (), ()
    if head_gain is not None:
        aux = (head_gain.astype(F32).reshape(1, HEAD_DIM),)
        aux_specs = (pl.BlockSpec((1, HEAD_DIM), lambda i, j, k: (0, 0)),)
    if transposed_chunk:
        tc = transposed_chunk
        out_shape = jax.ShapeDtypeStruct((d // HEAD_DIM, s // tc, HEAD_DIM, tc), BF16)
        out_specs = [pl.BlockSpec((hpt, tm // tc, HEAD_DIM, tc), lambda i, j, k: (j, i, 0, 0))]
    elif head_major:
        out_shape = jax.ShapeDtypeStruct((d // HEAD_DIM, s, HEAD_DIM), BF16)
        out_specs = [pl.BlockSpec((hpt, tm, HEAD_DIM), lambda i, j, k: (j, i, 0))]
    else:
        out_shape, out_specs = jax.ShapeDtypeStruct((s, d), BF16), None
    (out,) = _matmul(n, w, layer, tm=tm, tn=tn, tk=d, n_cols=d, w_col_block=lambda j: j + blk0,
                     epilogue=epilogue, aux=aux, aux_specs=aux_specs,
                     out_shapes=[out_shape], out_specs=out_specs, name=name)
    return out


def _hgrn_in_proj_forget(n, w_in, w_layer, lb_logits, layer):
    s, d = n.shape
    depth = lb_logits.shape[0]
    tm, tn = _tile(s, 1024), _tile(d, MATMUL_TN)
    nq = d // tn

    def epilogue(acc, aux, outs):
        (lb_ref,) = aux
        g_ref, k_ref = outs
        z = lb_ref[...]
        e = jnp.exp(z - jnp.max(z, axis=0, keepdims=True))
        p = e / jnp.sum(e, axis=0, keepdims=True)
        lb = jnp.zeros((1, tn), F32)
        for r in range(1, layer + 1):
            lb = lb + p[r:r + 1, :]
        f = lb + (1.0 - lb) * _sigmoid(acc)
        g_ref[...] = jnp.log(f)
        k_ref[...] = (1.0 - f).astype(k_ref.dtype)

    g, k = _matmul(n, w_in, w_layer, tm=tm, tn=tn, tk=d, n_cols=d, w_col_block=lambda j: j + nq,
                   epilogue=epilogue, aux=(lb_logits.astype(F32),),
                   aux_specs=(pl.BlockSpec((depth, tn), lambda i, j, k: (0, j)),),
                   out_shapes=[jax.ShapeDtypeStruct((s, d), F32), jax.ShapeDtypeStruct((s, d), BF16)],
                   name="hgrn_in_proj_forget")
    return g, k


def _proj_residual(x, w, layer, h, *, name):
    s, kdim = (x.shape[1], x.shape[0] * HEAD_DIM) if x.ndim == 3 else x.shape
    d = w.shape[2]
    tm = _tile(s, 1024)
    if kdim <= 4096:
        tn, tk = _tile(d, MATMUL_TN), kdim

        def epilogue(acc, aux, outs):
            (h_ref,) = aux
            (o_ref,) = outs
            o_ref[...] = h_ref[...] + acc
    else:
        tn, tk, epilogue = _tile(d, MATMUL_TN), 4096, None

    (out,) = _matmul(x, w, layer, tm=tm, tn=tn, tk=tk, n_cols=d, w_col_block=lambda j: j,
                     epilogue=epilogue, aux=(h,),
                     aux_specs=(pl.BlockSpec((tm, tn), lambda i, j, k: (i, j)),),
                     out_shapes=[jax.ShapeDtypeStruct((s, d), F32)], name=name)
    return out


def _mlp_up(n, w_up, layer):
    s, d = n.shape
    f = w_up.shape[2]
    tm, tn = _tile(s, 1024), _tile(f, MATMUL_TN)

    def epilogue(acc, aux, outs):
        (o_ref,) = outs
        r = jnp.maximum(acc, 0.0)
        o_ref[...] = (r * r).astype(o_ref.dtype)

    (out,) = _matmul(n, w_up, layer, tm=tm, tn=tn, tk=d, n_cols=f, w_col_block=lambda j: j,
                     epilogue=epilogue, aux=(), aux_specs=(),
                     out_shapes=[jax.ShapeDtypeStruct((s, f), BF16)], name="mlp_up")
    return out


def _fox_forget_kernel(x_ref, wf_ref, bias_ref, tri_ref, perm_ref, c_ref, kaux_ref, carry_ref):
    i = pl.program_id(0)

    @pl.when(i == 0)
    def _():
        carry_ref[...] = jnp.zeros_like(carry_ref)

    fz = jnp.dot(x_ref[...], wf_ref[...], preferred_element_type=F32)
    lf = _log_sigmoid(fz + bias_ref[...]) * LOG2E
    tri = tri_ref[...]
    cs = jnp.zeros_like(lf)
    for part in _split_bf16(lf, 3):
        cs = cs + jnp.dot(tri, part, preferred_element_type=F32)
    c = cs + carry_ref[...]
    c_ref[...] = c
    carry_ref[...] = c[c.shape[0] - 1:, :]

    pieces = jnp.concatenate(_split_bf16(c, AUX_PIECES), axis=1)
    lane = lax.broadcasted_iota(jnp.int32, (1, HEAD_DIM), 1)
    const = jnp.where(jnp.logical_and(lane >= AUX_CONST_LANE, lane < AUX_CONST_LANE + AUX_PIECES), 1.0, 0.0)
    kaux = jnp.dot(pieces, perm_ref[...], preferred_element_type=F32) + const
    kaux_ref[...] = kaux.astype(kaux_ref.dtype)


def _fox_forget_cumsum(n, w_f, bias):
    s, d = n.shape
    heads = w_f.shape[1]
    assert AUX_PIECES * heads <= AUX_CONST_LANE
    tm = _tile(s, 512)
    r = jnp.arange(tm)
    tri = (r[None, :] <= r[:, None]).astype(BF16)
    wf = jnp.zeros((d, HEAD_DIM), BF16).at[:, :heads].set(w_f)
    b = jnp.zeros((1, HEAD_DIM), F32).at[0, :heads].set(bias.astype(F32))
    hh = jnp.arange(heads)
    perm = jnp.zeros((AUX_PIECES * HEAD_DIM, HEAD_DIM), BF16)
    for p in range(AUX_PIECES):
        perm = perm.at[p * HEAD_DIM + hh, AUX_PIECES * hh + p].set(-1.0)
    return pl.pallas_call(
        _fox_forget_kernel,
        out_shape=[jax.ShapeDtypeStruct((s, HEAD_DIM), F32), jax.ShapeDtypeStruct((s, HEAD_DIM), BF16)],
        grid=(s // tm,),
        in_specs=[pl.BlockSpec((tm, d), lambda i: (i, 0)),
                  pl.BlockSpec((d, HEAD_DIM), lambda i: (0, 0)),
                  pl.BlockSpec((1, HEAD_DIM), lambda i: (0, 0)),
                  pl.BlockSpec((tm, tm), lambda i: (0, 0)),
                  pl.BlockSpec((AUX_PIECES * HEAD_DIM, HEAD_DIM), lambda i: (0, 0))],
        out_specs=[pl.BlockSpec((tm, HEAD_DIM), lambda i: (i, 0)),
                   pl.BlockSpec((tm, HEAD_DIM), lambda i: (i, 0))],
        scratch_shapes=[pltpu.VMEM((1, HEAD_DIM), F32)],
        compiler_params=_params("arbitrary"),
        name="fox_forget_cumsum",
    )(n, wf, b, tri, perm)


def _fox_attn_kernel(q_ref, k_ref, v_ref, kaux_ref, c_ref, gate_ref, o_ref,
                     st_a, st_b, cmax_a, cmax_b, p_a, p_b, acc_ref, *, tq, heads_per_step):
    i = pl.program_id(1)
    tk = tq // 2
    lane = lax.broadcasted_iota(jnp.int32, (1, HEAD_DIM), 1)
    hs = range(heads_per_step)

    q_aug = []
    for hh in hs:
        h = pl.program_id(0) * heads_per_step + hh
        c0 = jnp.sum(jnp.where(lane == h, c_ref[0:1, :], 0.0), axis=1, keepdims=True)
        qaux = jnp.where(jnp.logical_and(lane >= AUX_PIECES * h, lane < AUX_PIECES * (h + 1)), 1.0, 0.0)
        for p, piece in enumerate(_split_bf16(c0, AUX_PIECES)):
            qaux = jnp.where(lane == AUX_CONST_LANE + p, piece.astype(F32), qaux)
        q_aug.append(jnp.concatenate(
            [q_ref[hh], jnp.broadcast_to(qaux.astype(BF16), (tq, HEAD_DIM))], axis=1))

    def logits(hh, c):
        row0 = pl.multiple_of(c * tk, tk)
        k_aug = jnp.concatenate([k_ref[hh, pl.ds(row0, tk), :], kaux_ref[pl.ds(row0, tk), :]], axis=1)
        return lax.dot_general(k_aug, q_aug[hh], _NT, preferred_element_type=F32)

    def weighted_values(hh, c, p):
        return jnp.dot(v_ref[hh, c], p, preferred_element_type=F32)

    def fill(hh, c, st_buf, cmax_buf):
        st = logits(hh, c)
        st_buf[hh, :, :tq] = st
        cmax_buf[hh] = jnp.max(st, axis=0, keepdims=True)

    def step(hh, c, m, l, st_cur, cmax_cur, p_prev, p_cur, key0=None):
        acc = acc_ref[hh] + weighted_values(hh, jnp.maximum(c - 1, 0), p_prev[hh, :, :tq])
        st = st_cur[hh, :, :tq]
        if key0 is None:
            cmax = cmax_cur[hh]
            fill(hh, c + 2, st_cur, cmax_cur)
        else:
            keys = lax.broadcasted_iota(jnp.int32, (tk, tq), 0) + key0
            queries = lax.broadcasted_iota(jnp.int32, (tk, tq), 1)
            st = jnp.where(keys <= queries, st, NEG_BIG)
            cmax = jnp.max(st, axis=0, keepdims=True)
        m_new = jnp.maximum(m, cmax)
        alpha = jnp.exp2(m - m_new)
        p = jnp.exp2(st - m_new)
        l_new = alpha * l + jnp.sum(p, axis=0, keepdims=True)
        p_cur[hh, :, :tq] = p.astype(BF16)
        acc_ref[hh] = alpha * acc
        return m_new, l_new

    def steps(c, ml, st_cur, cmax_cur, p_prev, p_cur, key0=None):
        return tuple(step(hh, c, *ml[hh], st_cur, cmax_cur, p_prev, p_cur, key0) for hh in hs)

    for hh in hs:
        fill(hh, 0, st_a, cmax_a)
        fill(hh, 1, st_b, cmax_b)
    p_b[...] = jnp.zeros_like(p_b)
    acc_ref[...] = jnp.zeros_like(acc_ref)

    def pairs(c0, n_pairs, ml):
        for r in range(n_pairs):
            ml = steps(c0 + 2 * r, ml, st_a, cmax_a, p_b, p_a)
            ml = steps(c0 + 2 * r + 1, ml, st_b, cmax_b, p_a, p_b)
        return ml

    ml = tuple((jnp.full((1, tq), NEG_BIG, F32), jnp.zeros((1, tq), F32)) for _ in hs)
    n_trips = i // ATTN_PAIRS_PER_TRIP
    ml = lax.fori_loop(
        0, n_trips, lambda jj, c: pairs(2 * ATTN_PAIRS_PER_TRIP * jj, ATTN_PAIRS_PER_TRIP, c), ml)
    ml = lax.fori_loop(n_trips * ATTN_PAIRS_PER_TRIP, i, lambda jj, c: pairs(2 * jj, 1, c), ml)
    ml = steps(2 * i, ml, st_a, cmax_a, p_b, p_a, key0=0)
    ml = steps(2 * i + 1, ml, st_b, cmax_b, p_a, p_b, key0=tk)
    for hh in hs:
        acc = acc_ref[hh] + weighted_values(hh, 2 * i + 1, p_b[hh, :, :tq])
        o_ref[hh] = ((acc / ml[hh][1]).T * gate_ref[hh].astype(F32)).astype(o_ref.dtype)


def _fox_attention(q, k, v, gate, c, kaux):
    heads, s, _ = q.shape
    tq = _tile(s, ATTN_TQ)
    hps = ATTN_HEADS_PER_STEP if heads % ATTN_HEADS_PER_STEP == 0 else 1
    kern = functools.partial(_fox_attn_kernel, tq=tq, heads_per_step=hps)
    once = pl.Buffered(1)
    kv_spec = lambda: pl.BlockSpec((hps, s, HEAD_DIM), lambda h, i: (h, 0, 0), pipeline_mode=once)
    return pl.pallas_call(
        kern,
        out_shape=jax.ShapeDtypeStruct((heads, s, HEAD_DIM), BF16),
        grid=(heads // hps, s // tq),
        in_specs=[pl.BlockSpec((hps, tq, HEAD_DIM), lambda h, i: (h, i, 0)),
                  kv_spec(),
                  pl.BlockSpec((hps, s // (tq // 2), HEAD_DIM, tq // 2), lambda h, i: (h, 0, 0, 0),
                               pipeline_mode=once),
                  pl.BlockSpec((s, HEAD_DIM), lambda h, i: (0, 0), pipeline_mode=once),
                  pl.BlockSpec((8, HEAD_DIM), lambda h, i: (i * (tq // 8), 0)),
                  pl.BlockSpec((hps, tq, HEAD_DIM), lambda h, i: (h, i, 0))],
        out_specs=pl.BlockSpec((hps, tq, HEAD_DIM), lambda h, i: (h, i, 0)),
        scratch_shapes=[pltpu.VMEM((hps, tq // 2, tq + HEAD_DIM), F32),
                        pltpu.VMEM((hps, tq // 2, tq + HEAD_DIM), F32),
                        pltpu.VMEM((hps, 1, tq), F32), pltpu.VMEM((hps, 1, tq), F32),
                        pltpu.VMEM((hps, tq // 2, tq + HEAD_DIM), BF16),
                        pltpu.VMEM((hps, tq // 2, tq + HEAD_DIM), BF16),
                        pltpu.VMEM((hps, HEAD_DIM, tq), F32)],
        compiler_params=_params("parallel", "arbitrary"),
        name="fox_attention",
    )(q, k, v, kaux, c, gate)


def _hgrn_level_widths(t):
    ws = []
    w = t // 2
    while w >= HGRN_DIAG:
        ws.append(w)
        w //= 2
    return ws


def _hgrn_sum_masks(t):
    r = jnp.arange(t)[:, None]
    c = jnp.arange(t)[None, :]
    blocks = [c <= r]
    for w in _hgrn_level_widths(t):
        ref = (r // (2 * w)) * (2 * w) + w - 1
        q_side = (r % (2 * w)) >= w
        blocks.append(jnp.where(q_side, (c > ref) & (c <= r), (c > r) & (c <= ref)))
    return jnp.concatenate(blocks, axis=0).astype(BF16)


def _hgrn_scan_kernel(q_ref, k_ref, g_ref, v_ref, gate_ref, gain_ref, masks_ref, ones_ref,
                      o_ref, state_ref, *, t, heads_per_step):
    @pl.when(pl.program_id(1) == 0)
    def _():
        state_ref[...] = jnp.zeros_like(state_ref)

    for hh in range(heads_per_step):
        lanes = slice(hh * HEAD_DIM, (hh + 1) * HEAD_DIM)
        _hgrn_scan_head(q_ref.at[:, lanes], k_ref.at[:, lanes], g_ref.at[:, lanes], v_ref.at[:, lanes],
                        gate_ref.at[:, lanes], gain_ref, masks_ref, ones_ref, o_ref.at[:, lanes],
                        state_ref.at[hh], t)


def _hgrn_scan_head(q_ref, k_ref, g_ref, v_ref, gate_ref, gain_ref, masks_ref, ones_ref, o_ref,
                    state_ref, t):
    q = q_ref[...].astype(F32)
    k = k_ref[...].astype(F32)
    v_bf = v_ref[...]
    v = v_bf.astype(F32)
    g = g_ref[...]

    pieces = jnp.concatenate(_split_bf16(g, 2), axis=1)
    sums2 = jnp.dot(masks_ref[...], pieces, preferred_element_type=F32)
    sums = sums2[:, :HEAD_DIM] + sums2[:, HEAD_DIM:]
    b = sums[0:t]
    b_last = b[t - 1:t]

    st = state_ref[...]
    o = lax.dot_general((q * jnp.exp(b)).astype(BF16), st.astype(BF16), _NT, preferred_element_type=F32)

    rows = lax.broadcasted_iota(jnp.int32, (t, 1), 0)
    scores = jnp.zeros((t, t), F32)
    for lvl, w in enumerate(_hgrn_level_widths(t)):
        fac = jnp.exp(sums[(lvl + 1) * t:(lvl + 2) * t])
        q_side = (rows % (2 * w)) >= w
        a = jnp.where(q_side, q * fac, 0.0).astype(BF16)
        bk = jnp.where(q_side, 0.0, k * fac).astype(BF16)
        sc = lax.dot_general(a, bk, _NT, preferred_element_type=F32)
        if 2 * w < t:
            ri = lax.broadcasted_iota(jnp.int32, (t, t), 0) // (2 * w)
            cj = lax.broadcasted_iota(jnp.int32, (t, t), 1) // (2 * w)
            sc = jnp.where(ri == cj, sc, 0.0)
        scores = scores + sc
    o = o + jnp.dot(scores.astype(BF16), v_bf, preferred_element_type=F32)

    def group_roll(a, dd):
        return pltpu.roll(a.reshape(t // HGRN_DIAG, HGRN_DIAG, HEAD_DIM), dd, 1).reshape(t, HEAD_DIM)

    sub = rows % HGRN_DIAG
    rel = jnp.zeros_like(g)
    xs, vds = [q * k], [v]
    for dd in range(1, HGRN_DIAG):
        rel = rel + (group_roll(g, dd - 1) if dd > 1 else g)
        xs.append(q * group_roll(k, dd) * jnp.exp(jnp.where(sub >= dd, rel, NEG_BIG)))
        vds.append(group_roll(v, dd))
    ones2 = ones_ref[...]
    for dd in range(0, HGRN_DIAG, 2):
        pair = jnp.concatenate([xs[dd], xs[dd + 1]], axis=1).astype(BF16)
        rowsums = jnp.dot(pair, ones2, preferred_element_type=F32)
        o = o + rowsums[:, :HEAD_DIM] * vds[dd] + rowsums[:, HEAD_DIM:] * vds[dd + 1]

    kh = (k * jnp.exp(b_last - b)).astype(BF16)
    state_ref[...] = st * jnp.exp(b_last) + lax.dot_general(v_bf, kh, _TN, preferred_element_type=F32)

    ms = jnp.mean(o * o, axis=-1, keepdims=True)
    y = o * lax.rsqrt(ms + NORM_EPS) * gain_ref[...]
    o_ref[...] = (y * gate_ref[...].astype(F32)).astype(o_ref.dtype)


def _hgrn_scan(q, k, g, v, gate, out_gain):
    s, d = g.shape
    heads = d // HEAD_DIM
    t = _tile(s, HGRN_CHUNK)
    masks = _hgrn_sum_masks(t)
    nm = masks.shape[0]
    hps = HGRN_HEADS_PER_STEP if heads % HGRN_HEADS_PER_STEP == 0 else 1
    kern = functools.partial(_hgrn_scan_kernel, t=t, heads_per_step=hps)
    blk = lambda: pl.BlockSpec((t, hps * HEAD_DIM), lambda h, c: (c, h))
    return pl.pallas_call(
        kern,
        out_shape=jax.ShapeDtypeStruct((s, d), BF16),
        grid=(heads // hps, s // t),
        in_specs=[blk(), blk(), blk(), blk(), blk(),
                  pl.BlockSpec((1, HEAD_DIM), lambda h, c: (0, 0)),
                  pl.BlockSpec((nm, t), lambda h, c: (0, 0)),
                  pl.BlockSpec((2 * HEAD_DIM, 2 * HEAD_DIM), lambda h, c: (0, 0))],
        out_specs=blk(),
        scratch_shapes=[pltpu.VMEM((hps, HEAD_DIM, HEAD_DIM), F32)],
        compiler_params=_params("parallel", "arbitrary"),
        name="hgrn_scan",
    )(q, k, g, v, gate, out_gain.reshape(1, HEAD_DIM).astype(F32), masks,
      jnp.kron(jnp.eye(2, dtype=BF16), jnp.ones((HEAD_DIM, HEAD_DIM), BF16)))


def kernel(x, fox_w_in, fox_w_out, fox_q_gain, fox_k_gain, fox_fgate_bias, hgrn_w_in, hgrn_w_out,
           hgrn_out_gain, hgrn_lb_logits, mixer_norm_gain, mlp_norm_gain, mlp_w_up, mlp_w_down,
           final_norm_gain):
    batch, seq, d = x.shape
    depth = mixer_norm_gain.shape[0]
    fox_w_in_b, fox_w_out_b = fox_w_in[:, :, :4 * d].astype(BF16), fox_w_out.astype(BF16)
    fox_w_f_b = fox_w_in[:, :, 4 * d:].astype(BF16)
    hgrn_w_in_b, hgrn_w_out_b = hgrn_w_in.astype(BF16), hgrn_w_out.astype(BF16)
    mlp_w_up_b, mlp_w_down_b = mlp_w_up.astype(BF16), mlp_w_down.astype(BF16)
    q_scale = HEAD_DIM ** -0.5 * LOG2E
    outs = []
    for bi in range(batch):
        h = x[bi]
        for i in range(depth):
            j = i // 2
            n = _rmsnorm(h, mixer_norm_gain[i], BF16)
            if i % 2 == 0:
                q = _in_proj(n, fox_w_in_b, j, 0, "fox_in_proj_q", head_gain=fox_q_gain[j] * q_scale,
                             head_major=True)
                k = _in_proj(n, fox_w_in_b, j, d, "fox_in_proj_k", head_gain=fox_k_gain[j], head_major=True)
                v = _in_proj(n, fox_w_in_b, j, 2 * d, "fox_in_proj_v",
                             transposed_chunk=_tile(seq, ATTN_TQ) // 2)
                gate = _in_proj(n, fox_w_in_b, j, 3 * d, "fox_in_proj_gate", act=_sigmoid, head_major=True)
                c, kaux = _fox_forget_cumsum(n, fox_w_f_b[j], fox_fgate_bias[j])
                mix_in = _fox_attention(q, k, v, gate, c, kaux)
                h = _proj_residual(mix_in, fox_w_out_b, j, h, name="mixer_out_proj")
            else:
                q = _in_proj(n, hgrn_w_in_b, j, 0, "hgrn_in_proj_q", act=_silu)
                g, k = _hgrn_in_proj_forget(n, hgrn_w_in_b, j, hgrn_lb_logits, i)
                v = _in_proj(n, hgrn_w_in_b, j, 2 * d, "hgrn_in_proj_v")
                gate = _in_proj(n, hgrn_w_in_b, j, 3 * d, "hgrn_in_proj_gate", act=_silu)
                mix_in = _hgrn_scan(q, k, g, v, gate, hgrn_out_gain[j])
                h = _proj_residual(mix_in, hgrn_w_out_b, j, h, name="mixer_out_proj")
            n = _rmsnorm(h, mlp_norm_gain[i], BF16)
            a = _mlp_up(n, mlp_w_up_b, i)
            h = _proj_residual(a, mlp_w_down_b, i, h, name="mlp_down")
        outs.append(_rmsnorm(h, final_norm_gain, x.dtype))
    return jnp.stack(outs, axis=0)
```

```python
import functools

import jax
import jax.numpy as jnp
from jax import lax
from jax.experimental import pallas as pl
from jax.experimental.pallas import tpu as pltpu

F32 = jnp.float32
BF16 = jnp.bfloat16

HEAD_DIM = 128
NORM_EPS = 1e-6
NEG_BIG = -1e30
VMEM_LIMIT_BYTES = 56 * 1024 * 1024

MATMUL_TN = 1024
LOG2E = 1.4426950408889634
AUX_PIECES = 3
AUX_CONST_LANE = 96

ATTN_TQ = 1024
ATTN_HEADS_PER_STEP = 2
ATTN_PAIRS_PER_TRIP = 4

HGRN_CHUNK = 256
HGRN_HEADS_PER_STEP = 4
HGRN_DIAG = 8

_NT = (((1,), (1,)), ((), ()))
_TN = (((0,), (0,)), ((), ()))


def _tile(n, pref):
    t = min(n, pref)
    while n % t:
        t //= 2
    return t


def _params(*sem):
    return pltpu.CompilerParams(dimension_semantics=sem, vmem_limit_bytes=VMEM_LIMIT_BYTES)


def _split_bf16(a, terms):
    parts = []
    r = a
    for _ in range(terms):
        p = r.astype(BF16)
        parts.append(p)
        r = r - p.astype(F32)
    return parts


def _sigmoid(x):
    return 1.0 / (1.0 + jnp.exp(-x))


def _log_sigmoid(x):
    return jnp.minimum(x, 0.0) - jnp.log(1.0 + jnp.exp(-jnp.abs(x)))


def _rmsnorm_kernel(x_ref, g_ref, o_ref):
    x = x_ref[...]
    ms = jnp.mean(x * x, axis=-1, keepdims=True)
    o_ref[...] = (x * lax.rsqrt(ms + NORM_EPS) * g_ref[...]).astype(o_ref.dtype)


def _rmsnorm(x, gain, out_dtype):
    s, d = x.shape
    tm = _tile(s, 256)
    return pl.pallas_call(
        _rmsnorm_kernel,
        out_shape=jax.ShapeDtypeStruct((s, d), out_dtype),
        grid=(s // tm,),
        in_specs=[pl.BlockSpec((tm, d), lambda i: (i, 0)),
                  pl.BlockSpec((1, d), lambda i: (0, 0))],
        out_specs=pl.BlockSpec((tm, d), lambda i: (i, 0)),
        compiler_params=_params("parallel"),
        name="rmsnorm",
    )(x, gain.reshape(1, d).astype(F32))


def _matmul_kernel(x_ref, w_ref, *refs, nk, n_aux, epilogue):
    aux, outs = refs[:n_aux], refs[n_aux:]
    if nk == 1:
        if len(x_ref.shape) == 3:
            x = jnp.concatenate([x_ref[hd] for hd in range(x_ref.shape[0])], axis=1)
        else:
            x = x_ref[...]
        epilogue(jnp.dot(x, w_ref[...], preferred_element_type=F32), aux, outs)
        return
    (h_ref,), (o_ref,) = aux, outs

    @pl.when(pl.program_id(2) == 0)
    def _():
        o_ref[...] = h_ref[...]

    o_ref[...] += jnp.dot(x_ref[...], w_ref[...], preferred_element_type=F32)


def _matmul(x, w, layer, *, tm, tn, tk, n_cols, w_col_block, epilogue, aux, aux_specs, out_shapes, name,
            out_specs=None):
    if x.ndim == 3:
        heads, m, _ = x.shape
        kdim = heads * HEAD_DIM
        assert tk == kdim
        x_spec = pl.BlockSpec((heads, tm, HEAD_DIM), lambda i, j, k: (0, i, 0))
    else:
        m, kdim = x.shape
        x_spec = pl.BlockSpec((tm, tk), lambda i, j, k: (i, k))
    nk = kdim // tk
    if nk > 1:
        assert epilogue is None and len(aux) == 1 and len(out_shapes) == 1 and out_shapes[0].dtype == F32
    kern = functools.partial(_matmul_kernel, nk=nk, n_aux=len(aux), epilogue=epilogue)
    return pl.pallas_call(
        kern,
        out_shape=out_shapes,
        grid=(m // tm, n_cols // tn, nk),
        in_specs=[x_spec,
                  pl.BlockSpec((None, tk, tn), lambda i, j, k: (layer, k, w_col_block(j)))] + list(aux_specs),
        out_specs=out_specs or [pl.BlockSpec((tm, tn), lambda i, j, k: (i, j)) for _ in out_shapes],
        compiler_params=_params("parallel", "parallel", "arbitrary"),
        name=name,
    )(x, w, *aux)


def _silu(x):
    return x * _sigmoid(x)


def _in_proj(n, w, layer, col0, name, *, act=None, head_gain=None, head_major=False, transposed_chunk=None):
    s, d = n.shape
    tm, tn = _tile(s, 1024), _tile(d, MATMUL_TN)
    blk0 = col0 // tn
    hpt = tn // HEAD_DIM

    def epilogue(acc, aux, outs):
        (o_ref,) = outs
        for c in range(hpt):
            xc = acc[:, c * HEAD_DIM:(c + 1) * HEAD_DIM]
            if head_gain is not None:
                ms = jnp.mean(xc * xc, axis=-1, keepdims=True)
                xc = xc * lax.rsqrt(ms + NORM_EPS) * aux[0][...]
            elif act is not None:
                xc = act(xc)
            if transposed_chunk:
                for r in range(tm // transposed_chunk):
                    rows = slice(r * transposed_chunk, (r + 1) * transposed_chunk)
                    o_ref[c, r] = xc[rows, :].T.astype(o_ref.dtype)
            elif head_major:
                o_ref[c] = xc.astype(o_ref.dtype)
            else:
                o_ref[:, c * HEAD_DIM:(c + 1) * HEAD_DIM] = xc.astype(o_ref.dtype)

    aux, aux_specs = (), ()
    if head_gain is not None:
        aux = (head_gain.astype(F32).reshape(1, HEAD_DIM),)
        aux_specs = (pl.BlockSpec((1, HEAD_DIM), lambda i, j, k: (0, 0)),)
    if transposed_chunk:
        tc = transposed_chunk
        out_shape = jax.ShapeDtypeStruct((d // HEAD_DIM, s // tc, HEAD_DIM, tc), BF16)
        out_specs = [pl.BlockSpec((hpt, tm // tc, HEAD_DIM, tc), lambda i, j, k: (j, i, 0, 0))]
    elif head_major:
        out_shape = jax.ShapeDtypeStruct((d // HEAD_DIM, s, HEAD_DIM), BF16)
        out_specs = [pl.BlockSpec((hpt, tm, HEAD_DIM), lambda i, j, k: (j, i, 0))]
    else:
        out_shape, out_specs = jax.ShapeDtypeStruct((s, d), BF16), None
    (out,) = _matmul(n, w, layer, tm=tm, tn=tn, tk=d, n_cols=d, w_col_block=lambda j: j + blk0,
                     epilogue=epilogue, aux=aux, aux_specs=aux_specs,
                     out_shapes=[out_shape], out_specs=out_specs, name=name)
    return out


def _hgrn_in_proj_forget(n, w_in, w_layer, lb_logits, layer):
    s, d = n.shape
    depth = lb_logits.shape[0]
    tm, tn = _tile(s, 1024), _tile(d, MATMUL_TN)
    nq = d // tn

    def epilogue(acc, aux, outs):
        (lb_ref,) = aux
        g_ref, k_ref = outs
        z = lb_ref[...]
        e = jnp.exp(z - jnp.max(z, axis=0, keepdims=True))
        p = e / jnp.sum(e, axis=0, keepdims=True)
        lb = jnp.zeros((1, tn), F32)
        for r in range(1, layer + 1):
            lb = lb + p[r:r + 1, :]
        f = lb + (1.0 - lb) * _sigmoid(acc)
        g_ref[...] = jnp.log(f)
        k_ref[...] = (1.0 - f).astype(k_ref.dtype)

    g, k = _matmul(n, w_in, w_layer, tm=tm, tn=tn, tk=d, n_cols=d, w_col_block=lambda j: j + nq,
                   epilogue=epilogue, aux=(lb_logits.astype(F32),),
                   aux_specs=(pl.BlockSpec((depth, tn), lambda i, j, k: (0, j)),),
                   out_shapes=[jax.ShapeDtypeStruct((s, d), F32), jax.ShapeDtypeStruct((s, d), BF16)],
                   name="hgrn_in_proj_forget")
    return g, k


def _proj_residual(x, w, layer, h, *, name):
    s, kdim = (x.shape[1], x.shape[0] * HEAD_DIM) if x.ndim == 3 else x.shape
    d = w.shape[2]
    tm = _tile(s, 1024)
    if kdim <= 4096:
        tn, tk = _tile(d, MATMUL_TN), kdim

        def epilogue(acc, aux, outs):
            (h_ref,) = aux
            (o_ref,) = outs
            o_ref[...] = h_ref[...] + acc
    else:
        tn, tk, epilogue = _tile(d, MATMUL_TN), 4096, None

    (out,) = _matmul(x, w, layer, tm=tm, tn=tn, tk=tk, n_cols=d, w_col_block=lambda j: j,
                     epilogue=epilogue, aux=(h,),
                     aux_specs=(pl.BlockSpec((tm, tn), lambda i, j, k: (i, j)),),
                     out_shapes=[jax.ShapeDtypeStruct((s, d), F32)], name=name)
    return out


def _mlp_up(n, w_up, layer):
    s, d = n.shape
    f = w_up.shape[2]
    tm, tn = _tile(s, 1024), _tile(f, MATMUL_TN)

    def epilogue(acc, aux, outs):
        (o_ref,) = outs
        r = jnp.maximum(acc, 0.0)
        o_ref[...] = (r * r).astype(o_ref.dtype)

    (out,) = _matmul(n, w_up, layer, tm=tm, tn=tn, tk=d, n_cols=f, w_col_block=lambda j: j,
                     epilogue=epilogue, aux=(), aux_specs=(),
                     out_shapes=[jax.ShapeDtypeStruct((s, f), BF16)], name="mlp_up")
    return out


def _fox_forget_kernel(x_ref, wf_ref, bias_ref, tri_ref, perm_ref, c_ref, kaux_ref, carry_ref):
    i = pl.program_id(0)

    @pl.when(i == 0)
    def _():
        carry_ref[...] = jnp.zeros_like(carry_ref)

    fz = jnp.dot(x_ref[...], wf_ref[...], preferred_element_type=F32)
    lf = _log_sigmoid(fz + bias_ref[...]) * LOG2E
    tri = tri_ref[...]
    cs = jnp.zeros_like(lf)
    for part in _split_bf16(lf, 3):
        cs = cs + jnp.dot(tri, part, preferred_element_type=F32)
    c = cs + carry_ref[...]
    c_ref[...] = c
    carry_ref[...] = c[c.shape[0] - 1:, :]

    pieces = jnp.concatenate(_split_bf16(c, AUX_PIECES), axis=1)
    lane = lax.broadcasted_iota(jnp.int32, (1, HEAD_DIM), 1)
    const = jnp.where(jnp.logical_and(lane >= AUX_CONST_LANE, lane < AUX_CONST_LANE + AUX_PIECES), 1.0, 0.0)
    kaux = jnp.dot(pieces, perm_ref[...], preferred_element_type=F32) + const
    kaux_ref[...] = kaux.astype(kaux_ref.dtype)


def _fox_forget_cumsum(n, w_f, bias):
    s, d = n.shape
    heads = w_f.shape[1]
    assert AUX_PIECES * heads <= AUX_CONST_LANE
    tm = _tile(s, 512)
    r = jnp.arange(tm)
    tri = (r[None, :] <= r[:, None]).astype(BF16)
    wf = jnp.zeros((d, HEAD_DIM), BF16).at[:, :heads].set(w_f)
    b = jnp.zeros((1, HEAD_DIM), F32).at[0, :heads].set(bias.astype(F32))
    hh = jnp.arange(heads)
    perm = jnp.zeros((AUX_PIECES * HEAD_DIM, HEAD_DIM), BF16)
    for p in range(AUX_PIECES):
        perm = perm.at[p * HEAD_DIM + hh, AUX_PIECES * hh + p].set(-1.0)
    return pl.pallas_call(
        _fox_forget_kernel,
        out_shape=[jax.ShapeDtypeStruct((s, HEAD_DIM), F32), jax.ShapeDtypeStruct((s, HEAD_DIM), BF16)],
        grid=(s // tm,),
        in_specs=[pl.BlockSpec((tm, d), lambda i: (i, 0)),
                  pl.BlockSpec((d, HEAD_DIM), lambda i: (0, 0)),
                  pl.BlockSpec((1, HEAD_DIM), lambda i: (0, 0)),
                  pl.BlockSpec((tm, tm), lambda i: (0, 0)),
                  pl.BlockSpec((AUX_PIECES * HEAD_DIM, HEAD_DIM), lambda i: (0, 0))],
        out_specs=[pl.BlockSpec((tm, HEAD_DIM), lambda i: (i, 0)),
                   pl.BlockSpec((tm, HEAD_DIM), lambda i: (i, 0))],
        scratch_shapes=[pltpu.VMEM((1, HEAD_DIM), F32)],
        compiler_params=_params("arbitrary"),
        name="fox_forget_cumsum",
    )(n, wf, b, tri, perm)


def _fox_attn_kernel(q_ref, k_ref, v_ref, kaux_ref, c_ref, gate_ref, o_ref,
                     st_a, st_b, cmax_a, cmax_b, p_a, p_b, acc_ref, *, tq, heads_per_step):
    i = pl.program_id(1)
    tk = tq // 2
    lane = lax.broadcasted_iota(jnp.int32, (1, HEAD_DIM), 1)
    hs = range(heads_per_step)

    q_aug = []
    for hh in hs:
        h = pl.program_id(0) * heads_per_step + hh
        c0 = jnp.sum(jnp.where(lane == h, c_ref[0:1, :], 0.0), axis=1, keepdims=True)
        qaux = jnp.where(jnp.logical_and(lane >= AUX_PIECES * h, lane < AUX_PIECES * (h + 1)), 1.0, 0.0)
        for p, piece in enumerate(_split_bf16(c0, AUX_PIECES)):
            qaux = jnp.where(lane == AUX_CONST_LANE + p, piece.astype(F32), qaux)
        q_aug.append(jnp.concatenate(
            [q_ref[hh], jnp.broadcast_to(qaux.astype(BF16), (tq, HEAD_DIM))], axis=1))

    def logits(hh, c):
        row0 = pl.multiple_of(c * tk, tk)
        k_aug = jnp.concatenate([k_ref[hh, pl.ds(row0, tk), :], kaux_ref[pl.ds(row0, tk), :]], axis=1)
        return lax.dot_general(k_aug, q_aug[hh], _NT, preferred_element_type=F32)

    def weighted_values(hh, c, p):
        return jnp.dot(v_ref[hh, c], p, preferred_element_type=F32)

    def fill(hh, c, st_buf, cmax_buf):
        st = logits(hh, c)
        st_buf[hh, :, :tq] = st
        cmax_buf[hh] = jnp.max(st, axis=0, keepdims=True)

    def step(hh, c, m, l, st_cur, cmax_cur, p_prev, p_cur, key0=None):
        acc = acc_ref[hh, :, :tq] + weighted_values(hh, jnp.maximum(c - 1, 0), p_prev[hh, :, :tq])
        st = st_cur[hh, :, :tq]
        if key0 is None:
            cmax = cmax_cur[hh]
            fill(hh, c + 2, st_cur, cmax_cur)
        else:
            keys = lax.broadcasted_iota(jnp.int32, (tk, tq), 0) + key0
            queries = lax.broadcasted_iota(jnp.int32, (tk, tq), 1)
            st = jnp.where(keys <= queries, st, NEG_BIG)
            cmax = jnp.max(st, axis=0, keepdims=True)
        m_new = jnp.maximum(m, cmax)
        alpha = jnp.exp2(m - m_new)
        p = jnp.exp2(st - m_new)
        l_new = alpha * l + jnp.sum(p, axis=0, keepdims=True)
        p_cur[hh, :, :tq] = p.astype(BF16)
        acc_ref[hh, :, :tq] = alpha * acc
        return m_new, l_new

    def steps(c, ml, st_cur, cmax_cur, p_prev, p_cur, key0=None):
        return tuple(step(hh, c, *ml[hh], st_cur, cmax_cur, p_prev, p_cur, key0) for hh in hs)

    for hh in hs:
        fill(hh, 0, st_a, cmax_a)
        fill(hh, 1, st_b, cmax_b)
    p_b[...] = jnp.zeros_like(p_b)
    acc_ref[...] = jnp.zeros_like(acc_ref)

    def pairs(c0, n_pairs, ml):
        for r in range(n_pairs):
            ml = steps(c0 + 2 * r, ml, st_a, cmax_a, p_b, p_a)
            ml = steps(c0 + 2 * r + 1, ml, st_b, cmax_b, p_a, p_b)
        return ml

    ml = tuple((jnp.full((1, tq), NEG_BIG, F32), jnp.zeros((1, tq), F32)) for _ in hs)
    n_trips = i // ATTN_PAIRS_PER_TRIP
    ml = lax.fori_loop(
        0, n_trips, lambda jj, c: pairs(2 * ATTN_PAIRS_PER_TRIP * jj, ATTN_PAIRS_PER_TRIP, c), ml)
    ml = lax.fori_loop(n_trips * ATTN_PAIRS_PER_TRIP, i, lambda jj, c: pairs(2 * jj, 1, c), ml)
    ml = steps(2 * i, ml, st_a, cmax_a, p_b, p_a, key0=0)
    ml = steps(2 * i + 1, ml, st_b, cmax_b, p_a, p_b, key0=tk)
    for hh in hs:
        acc = acc_ref[hh, :, :tq] + weighted_values(hh, 2 * i + 1, p_b[hh, :, :tq])
        o_ref[hh] = ((acc / ml[hh][1]).T * gate_ref[hh].astype(F32)).astype(o_ref.dtype)


def _fox_attention(q, k, v, gate, c, kaux):
    heads, s, _ = q.shape
    tq = _tile(s, ATTN_TQ)
    hps = ATTN_HEADS_PER_STEP if heads % ATTN_HEADS_PER_STEP == 0 else 1
    kern = functools.partial(_fox_attn_kernel, tq=tq, heads_per_step=hps)
    once = pl.Buffered(1)
    kv_spec = lambda: pl.BlockSpec((hps, s, HEAD_DIM), lambda h, i: (h, 0, 0), pipeline_mode=once)
    return pl.pallas_call(
        kern,
        out_shape=jax.ShapeDtypeStruct((heads, s, HEAD_DIM), BF16),
        grid=(heads // hps, s // tq),
        in_specs=[pl.BlockSpec((hps, tq, HEAD_DIM), lambda h, i: (h, i, 0)),
                  kv_spec(),
                  pl.BlockSpec((hps, s // (tq // 2), HEAD_DIM, tq // 2), lambda h, i: (h, 0, 0, 0),
                               pipeline_mode=once),
                  pl.BlockSpec((s, HEAD_DIM), lambda h, i: (0, 0), pipeline_mode=once),
                  pl.BlockSpec((8, HEAD_DIM), lambda h, i: (i * (tq // 8), 0)),
                  pl.BlockSpec((hps, tq, HEAD_DIM), lambda h, i: (h, i, 0))],
        out_specs=pl.BlockSpec((hps, tq, HEAD_DIM), lambda h, i: (h, i, 0)),
        scratch_shapes=[pltpu.VMEM((hps, tq // 2, tq + HEAD_DIM), F32),
                        pltpu.VMEM((hps, tq // 2, tq + HEAD_DIM), F32),
                        pltpu.VMEM((hps, 1, tq), F32), pltpu.VMEM((hps, 1, tq), F32),
                        pltpu.VMEM((hps, tq // 2, tq + HEAD_DIM), BF16),
                        pltpu.VMEM((hps, tq // 2, tq + HEAD_DIM), BF16),
                        pltpu.VMEM((hps, HEAD_DIM, tq + HEAD_DIM), F32)],
        compiler_params=_params("parallel", "arbitrary"),
        name="fox_attention",
    )(q, k, v, kaux, c, gate)


def _hgrn_level_widths(t):
    ws = []
    w = t // 2
    while w >= HGRN_DIAG:
        ws.append(w)
        w //= 2
    return ws


def _hgrn_sum_masks(t):
    r = jnp.arange(t)[:, None]
    c = jnp.arange(t)[None, :]
    blocks = [c <= r]
    for w in _hgrn_level_widths(t):
        ref = (r // (2 * w)) * (2 * w) + w - 1
        q_side = (r % (2 * w)) >= w
        blocks.append(jnp.where(q_side, (c > ref) & (c <= r), (c > r) & (c <= ref)))
    return jnp.concatenate(blocks, axis=0).astype(BF16)


def _hgrn_scan_kernel(q_ref, k_ref, g_ref, v_ref, gate_ref, gain_ref, masks_ref, ones_ref,
                      o_ref, state_ref, *, t, heads_per_step):
    @pl.when(pl.program_id(1) == 0)
    def _():
        state_ref[...] = jnp.zeros_like(state_ref)

    for hh in range(heads_per_step):
        lanes = slice(hh * HEAD_DIM, (hh + 1) * HEAD_DIM)
        _hgrn_scan_head(q_ref.at[:, lanes], k_ref.at[:, lanes], g_ref.at[:, lanes], v_ref.at[:, lanes],
                        gate_ref.at[:, lanes], gain_ref, masks_ref, ones_ref, o_ref.at[:, lanes],
                        state_ref.at[hh], t)


def _hgrn_scan_head(q_ref, k_ref, g_ref, v_ref, gate_ref, gain_ref, masks_ref, ones_ref, o_ref,
                    state_ref, t):
    q = q_ref[...].astype(F32)
    k = k_ref[...].astype(F32)
    v_bf = v_ref[...]
    v = v_bf.astype(F32)
    g = g_ref[...]

    pieces = jnp.concatenate(_split_bf16(g, 2), axis=1)
    sums2 = jnp.dot(masks_ref[...], pieces, preferred_element_type=F32)
    sums = sums2[:, :HEAD_DIM] + sums2[:, HEAD_DIM:]
    b = sums[0:t]
    b_last = b[t - 1:t]

    st = state_ref[...]
    o = lax.dot_general((q * jnp.exp(b)).astype(BF16), st.astype(BF16), _NT, preferred_element_type=F32)

    rows = lax.broadcasted_iota(jnp.int32, (t, 1), 0)
    scores = jnp.zeros((t, t), F32)
    for lvl, w in enumerate(_hgrn_level_widths(t)):
        fac = jnp.exp(sums[(lvl + 1) * t:(lvl + 2) * t])
        q_side = (rows % (2 * w)) >= w
        a = jnp.where(q_side, q * fac, 0.0).astype(BF16)
        bk = jnp.where(q_side, 0.0, k * fac).astype(BF16)
        sc = lax.dot_general(a, bk, _NT, preferred_element_type=F32)
        if 2 * w < t:
            ri = lax.broadcasted_iota(jnp.int32, (t, t), 0) // (2 * w)
            cj = lax.broadcasted_iota(jnp.int32, (t, t), 1) // (2 * w)
            sc = jnp.where(ri == cj, sc, 0.0)
        scores = scores + sc
    o = o + jnp.dot(scores.astype(BF16), v_bf, preferred_element_type=F32)

    def group_roll(a, dd):
        return pltpu.roll(a.reshape(t // HGRN_DIAG, HGRN_DIAG, HEAD_DIM), dd, 1).reshape(t, HEAD_DIM)

    sub = rows % HGRN_DIAG
    rel = jnp.zeros_like(g)
    xs, vds = [q * k], [v]
    for dd in range(1, HGRN_DIAG):
        rel = rel + (group_roll(g, dd - 1) if dd > 1 else g)
        xs.append(q * group_roll(k, dd) * jnp.exp(jnp.where(sub >= dd, rel, NEG_BIG)))
        vds.append(group_roll(v, dd))
    ones2 = ones_ref[...]
    for dd in range(0, HGRN_DIAG, 2):
        pair = jnp.concatenate([xs[dd], xs[dd + 1]], axis=1).astype(BF16)
        rowsums = jnp.dot(pair, ones2, preferred_element_type=F32)
        o = o + rowsums[:, :HEAD_DIM] * vds[dd] + rowsums[:, HEAD_DIM:] * vds[dd + 1]

    kh = (k * jnp.exp(b_last - b)).astype(BF16)
    state_ref[...] = st * jnp.exp(b_last) + lax.dot_general(v_bf, kh, _TN, preferred_element_type=F32)

    ms = jnp.mean(o * o, axis=-1, keepdims=True)
    y = o * lax.rsqrt(ms + NORM_EPS) * gain_ref[...]
    o_ref[...] = (y * gate_ref[...].astype(F32)).astype(o_ref.dtype)


def _hgrn_scan(q, k, g, v, gate, out_gain):
    s, d = g.shape
    heads = d // HEAD_DIM
    t = _tile(s, HGRN_CHUNK)
    masks = _hgrn_sum_masks(t)
    nm = masks.shape[0]
    hps = HGRN_HEADS_PER_STEP if heads % HGRN_HEADS_PER_STEP == 0 else 1
    kern = functools.partial(_hgrn_scan_kernel, t=t, heads_per_step=hps)
    blk = lambda: pl.BlockSpec((t, hps * HEAD_DIM), lambda h, c: (c, h))
    return pl.pallas_call(
        kern,
        out_shape=jax.ShapeDtypeStruct((s, d), BF16),
        grid=(heads // hps, s // t),
        in_specs=[blk(), blk(), blk(), blk(), blk(),
                  pl.BlockSpec((1, HEAD_DIM), lambda h, c: (0, 0)),
                  pl.BlockSpec((nm, t), lambda h, c: (0, 0)),
                  pl.BlockSpec((2 * HEAD_DIM, 2 * HEAD_DIM), lambda h, c: (0, 0))],
        out_specs=blk(),
        scratch_shapes=[pltpu.VMEM((hps, HEAD_DIM, HEAD_DIM), F32)],
        compiler_params=_params("parallel", "arbitrary"),
        name="hgrn_scan",
    )(q, k, g, v, gate, out_gain.reshape(1, HEAD_DIM).astype(F32), masks,
      jnp.kron(jnp.eye(2, dtype=BF16), jnp.ones((HEAD_DIM, HEAD_DIM), BF16)))


def kernel(x, fox_w_in, fox_w_out, fox_q_gain, fox_k_gain, fox_fgate_bias, hgrn_w_in, hgrn_w_out,
           hgrn_out_gain, hgrn_lb_logits, mixer_norm_gain, mlp_norm_gain, mlp_w_up, mlp_w_down,
           final_norm_gain):
    batch, seq, d = x.shape
    depth = mixer_norm_gain.shape[0]
    fox_w_in_b, fox_w_out_b = fox_w_in[:, :, :4 * d].astype(BF16), fox_w_out.astype(BF16)
    fox_w_f_b = fox_w_in[:, :, 4 * d:].astype(BF16)
    hgrn_w_in_b, hgrn_w_out_b = hgrn_w_in.astype(BF16), hgrn_w_out.astype(BF16)
    mlp_w_up_b, mlp_w_down_b = mlp_w_up.astype(BF16), mlp_w_down.astype(BF16)
    q_scale = HEAD_DIM ** -0.5 * LOG2E
    outs = []
    for bi in range(batch):
        h = x[bi]
        for i in range(depth):
            j = i // 2
            n = _rmsnorm(h, mixer_norm_gain[i], BF16)
            if i % 2 == 0:
                q = _in_proj(n, fox_w_in_b, j, 0, "fox_in_proj_q", head_gain=fox_q_gain[j] * q_scale,
                             head_major=True)
                k = _in_proj(n, fox_w_in_b, j, d, "fox_in_proj_k", head_gain=fox_k_gain[j], head_major=True)
                v = _in_proj(n, fox_w_in_b, j, 2 * d, "fox_in_proj_v",
                             transposed_chunk=_tile(seq, ATTN_TQ) // 2)
                gate = _in_proj(n, fox_w_in_b, j, 3 * d, "fox_in_proj_gate", act=_sigmoid, head_major=True)
                c, kaux = _fox_forget_cumsum(n, fox_w_f_b[j], fox_fgate_bias[j])
                mix_in = _fox_attention(q, k, v, gate, c, kaux)
                h = _proj_residual(mix_in, fox_w_out_b, j, h, name="mixer_out_proj")
            else:
                q = _in_proj(n, hgrn_w_in_b, j, 0, "hgrn_in_proj_q", act=_silu)
                g, k = _hgrn_in_proj_forget(n, hgrn_w_in_b, j, hgrn_lb_logits, i)
                v = _in_proj(n, hgrn_w_in_b, j, 2 * d, "hgrn_in_proj_v")
                gate = _in_proj(n, hgrn_w_in_b, j, 3 * d, "hgrn_in_proj_gate", act=_silu)
                mix_in = _hgrn_scan(q, k, g, v, gate, hgrn_out_gain[j])
                h = _proj_residual(mix_in, hgrn_w_out_b, j, h, name="mixer_out_proj")
            n = _rmsnorm(h, mlp_norm_gain[i], BF16)
            a = _mlp_up(n, mlp_w_up_b, i)
            h = _proj_residual(a, mlp_w_down_b, i, h, name="mlp_down")
        outs.append(_rmsnorm(h, final_norm_gain, x.dtype))
    return jnp.stack(outs, axis=0)
```

```python
import functools

import jax
import jax.numpy as jnp
from jax import lax
from jax.experimental import pallas as pl
from jax.experimental.pallas import tpu as pltpu

F32 = jnp.float32
BF16 = jnp.bfloat16

HEAD_DIM = 128
NORM_EPS = 1e-6
NEG_BIG = -1e30
VMEM_LIMIT_BYTES = 56 * 1024 * 1024

MATMUL_TN = 1024
LOG2E = 1.4426950408889634
AUX_PIECES = 3
AUX_CONST_LANE = 96

ATTN_TQ = 1024
ATTN_HEADS_PER_STEP = 2
ATTN_PAIRS_PER_TRIP = 4

HGRN_CHUNK = 256
HGRN_HEADS_PER_STEP = 4
HGRN_DIAG = 8

_NT = (((1,), (1,)), ((), ()))
_TN = (((0,), (0,)), ((), ()))


def _tile(n, pref):
    t = min(n, pref)
    while n % t:
        t //= 2
    return t


def _params(*sem):
    return pltpu.CompilerParams(dimension_semantics=sem, vmem_limit_bytes=VMEM_LIMIT_BYTES)


def _split_bf16(a, terms):
    parts = []
    r = a
    for _ in range(terms):
        p = r.astype(BF16)
        parts.append(p)
        r = r - p.astype(F32)
    return parts


def _sigmoid(x):
    return 1.0 / (1.0 + jnp.exp(-x))


def _log_sigmoid(x):
    return jnp.minimum(x, 0.0) - jnp.log(1.0 + jnp.exp(-jnp.abs(x)))


def _rmsnorm_kernel(x_ref, g_ref, o_ref):
    x = x_ref[...]
    ms = jnp.mean(x * x, axis=-1, keepdims=True)
    o_ref[...] = (x * lax.rsqrt(ms + NORM_EPS) * g_ref[...]).astype(o_ref.dtype)


def _rmsnorm(x, gain, out_dtype):
    s, d = x.shape
    tm = _tile(s, 256)
    return pl.pallas_call(
        _rmsnorm_kernel,
        out_shape=jax.ShapeDtypeStruct((s, d), out_dtype),
        grid=(s // tm,),
        in_specs=[pl.BlockSpec((tm, d), lambda i: (i, 0)),
                  pl.BlockSpec((1, d), lambda i: (0, 0))],
        out_specs=pl.BlockSpec((tm, d), lambda i: (i, 0)),
        compiler_params=_params("parallel"),
        name="rmsnorm",
    )(x, gain.reshape(1, d).astype(F32))


def _matmul_kernel(x_ref, w_ref, *refs, nk, n_aux, epilogue):
    aux, outs = refs[:n_aux], refs[n_aux:]
    if nk == 1:
        if len(x_ref.shape) == 3:
            x = jnp.concatenate([x_ref[hd] for hd in range(x_ref.shape[0])], axis=1)
        else:
            x = x_ref[...]
        epilogue(jnp.dot(x, w_ref[...], preferred_element_type=F32), aux, outs)
        return
    (h_ref,), (o_ref,) = aux, outs

    @pl.when(pl.program_id(2) == 0)
    def _():
        o_ref[...] = h_ref[...]

    o_ref[...] += jnp.dot(x_ref[...], w_ref[...], preferred_element_type=F32)


def _matmul(x, w, layer, *, tm, tn, tk, n_cols, w_col_block, epilogue, aux, aux_specs, out_shapes, name,
            out_specs=None):
    if x.ndim == 3:
        heads, m, _ = x.shape
        kdim = heads * HEAD_DIM
        assert tk == kdim
        x_spec = pl.BlockSpec((heads, tm, HEAD_DIM), lambda i, j, k: (0, i, 0))
    else:
        m, kdim = x.shape
        x_spec = pl.BlockSpec((tm, tk), lambda i, j, k: (i, k))
    nk = kdim // tk
    if nk > 1:
        assert epilogue is None and len(aux) == 1 and len(out_shapes) == 1 and out_shapes[0].dtype == F32
    kern = functools.partial(_matmul_kernel, nk=nk, n_aux=len(aux), epilogue=epilogue)
    return pl.pallas_call(
        kern,
        out_shape=out_shapes,
        grid=(m // tm, n_cols // tn, nk),
        in_specs=[x_spec,
                  pl.BlockSpec((None, tk, tn), lambda i, j, k: (layer, k, w_col_block(j)))] + list(aux_specs),
        out_specs=out_specs or [pl.BlockSpec((tm, tn), lambda i, j, k: (i, j)) for _ in out_shapes],
        compiler_params=_params("parallel", "parallel", "arbitrary"),
        name=name,
    )(x, w, *aux)


def _silu(x):
    return x * _sigmoid(x)


def _in_proj(n, w, layer, col0, name, *, act=None, head_gain=None, head_major=False, transposed_chunk=None):
    s, d = n.shape
    tm, tn = _tile(s, 1024), _tile(d, MATMUL_TN)
    blk0 = col0 // tn
    hpt = tn // HEAD_DIM

    def epilogue(acc, aux, outs):
        (o_ref,) = outs
        for c in range(hpt):
            xc = acc[:, c * HEAD_DIM:(c + 1) * HEAD_DIM]
            if head_gain is not None:
                ms = jnp.mean(xc * xc, axis=-1, keepdims=True)
                xc = xc * lax.rsqrt(ms + NORM_EPS) * aux[0][...]
            elif act is not None:
                xc = act(xc)
            if transposed_chunk:
                for r in range(tm // transposed_chunk):
                    rows = slice(r * transposed_chunk, (r + 1) * transposed_chunk)
                    o_ref[c, r] = xc[rows, :].T.astype(o_ref.dtype)
            elif head_major:
                o_ref[c] = xc.astype(o_ref.dtype)
            else:
                o_ref[:, c * HEAD_DIM:(c + 1) * HEAD_DIM] = xc.astype(o_ref.dtype)

    aux, aux_specs = (), ()
    if head_gain is not None:
        aux = (head_gain.astype(F32).reshape(1, HEAD_DIM),)
        aux_specs = (pl.BlockSpec((1, HEAD_DIM), lambda i, j, k: (0, 0)),)
    if transposed_chunk:
        tc = transposed_chunk
        out_shape = jax.ShapeDtypeStruct((d // HEAD_DIM, s // tc, HEAD_DIM, tc), BF16)
        out_specs = [pl.BlockSpec((hpt, tm // tc, HEAD_DIM, tc), lambda i, j, k: (j, i, 0, 0))]
    elif head_major:
        out_shape = jax.ShapeDtypeStruct((d // HEAD_DIM, s, HEAD_DIM), BF16)
        out_specs = [pl.BlockSpec((hpt, tm, HEAD_DIM), lambda i, j, k: (j, i, 0))]
    else:
        out_shape, out_specs = jax.ShapeDtypeStruct((s, d), BF16), None
    (out,) = _matmul(n, w, layer, tm=tm, tn=tn, tk=d, n_cols=d, w_col_block=lambda j: j + blk0,
                     epilogue=epilogue, aux=aux, aux_specs=aux_specs,
                     out_shapes=[out_shape], out_specs=out_specs, name=name)
    return out


def _hgrn_in_proj_forget(n, w_in, w_layer, lb_logits, layer):
    s, d = n.shape
    depth = lb_logits.shape[0]
    tm, tn = _tile(s, 1024), _tile(d, MATMUL_TN)
    nq = d // tn

    def epilogue(acc, aux, outs):
        (lb_ref,) = aux
        g_ref, k_ref = outs
        z = lb_ref[...]
        e = jnp.exp(z - jnp.max(z, axis=0, keepdims=True))
        p = e / jnp.sum(e, axis=0, keepdims=True)
        lb = jnp.zeros((1, tn), F32)
        for r in range(1, layer + 1):
            lb = lb + p[r:r + 1, :]
        f = lb + (1.0 - lb) * _sigmoid(acc)
        g_ref[...] = jnp.log(f)
        k_ref[...] = (1.0 - f).astype(k_ref.dtype)

    g, k = _matmul(n, w_in, w_layer, tm=tm, tn=tn, tk=d, n_cols=d, w_col_block=lambda j: j + nq,
                   epilogue=epilogue, aux=(lb_logits.astype(F32),),
                   aux_specs=(pl.BlockSpec((depth, tn), lambda i, j, k: (0, j)),),
                   out_shapes=[jax.ShapeDtypeStruct((s, d), F32), jax.ShapeDtypeStruct((s, d), BF16)],
                   name="hgrn_in_proj_forget")
    return g, k


def _proj_residual(x, w, layer, h, *, name):
    s, kdim = (x.shape[1], x.shape[0] * HEAD_DIM) if x.ndim == 3 else x.shape
    d = w.shape[2]
    tm = _tile(s, 1024)
    if kdim <= 4096:
        tn, tk = _tile(d, MATMUL_TN), kdim

        def epilogue(acc, aux, outs):
            (h_ref,) = aux
            (o_ref,) = outs
            o_ref[...] = h_ref[...] + acc
    else:
        tn, tk, epilogue = _tile(d, MATMUL_TN), 4096, None

    (out,) = _matmul(x, w, layer, tm=tm, tn=tn, tk=tk, n_cols=d, w_col_block=lambda j: j,
                     epilogue=epilogue, aux=(h,),
                     aux_specs=(pl.BlockSpec((tm, tn), lambda i, j, k: (i, j)),),
                     out_shapes=[jax.ShapeDtypeStruct((s, d), F32)], name=name)
    return out


def _mlp_up(n, w_up, layer):
    s, d = n.shape
    f = w_up.shape[2]
    tm, tn = _tile(s, 1024), _tile(f, MATMUL_TN)

    def epilogue(acc, aux, outs):
        (o_ref,) = outs
        r = jnp.maximum(acc, 0.0)
        o_ref[...] = (r * r).astype(o_ref.dtype)

    (out,) = _matmul(n, w_up, layer, tm=tm, tn=tn, tk=d, n_cols=f, w_col_block=lambda j: j,
                     epilogue=epilogue, aux=(), aux_specs=(),
                     out_shapes=[jax.ShapeDtypeStruct((s, f), BF16)], name="mlp_up")
    return out


def _fox_forget_kernel(x_ref, wf_ref, bias_ref, tri_ref, perm_ref, c_ref, kaux_ref, carry_ref):
    i = pl.program_id(0)

    @pl.when(i == 0)
    def _():
        carry_ref[...] = jnp.zeros_like(carry_ref)

    fz = jnp.dot(x_ref[...], wf_ref[...], preferred_element_type=F32)
    lf = _log_sigmoid(fz + bias_ref[...]) * LOG2E
    tri = tri_ref[...]
    cs = jnp.zeros_like(lf)
    for part in _split_bf16(lf, 3):
        cs = cs + jnp.dot(tri, part, preferred_element_type=F32)
    c = cs + carry_ref[...]
    c_ref[...] = c
    carry_ref[...] = c[c.shape[0] - 1:, :]

    pieces = jnp.concatenate(_split_bf16(c, AUX_PIECES), axis=1)
    lane = lax.broadcasted_iota(jnp.int32, (1, HEAD_DIM), 1)
    const = jnp.where(jnp.logical_and(lane >= AUX_CONST_LANE, lane < AUX_CONST_LANE + AUX_PIECES), 1.0, 0.0)
    kaux = jnp.dot(pieces, perm_ref[...], preferred_element_type=F32) + const
    kaux_ref[...] = kaux.astype(kaux_ref.dtype)


def _fox_forget_cumsum(n, w_f, bias):
    s, d = n.shape
    heads = w_f.shape[1]
    assert AUX_PIECES * heads <= AUX_CONST_LANE
    tm = _tile(s, 512)
    r = jnp.arange(tm)
    tri = (r[None, :] <= r[:, None]).astype(BF16)
    wf = jnp.zeros((d, HEAD_DIM), BF16).at[:, :heads].set(w_f)
    b = jnp.zeros((1, HEAD_DIM), F32).at[0, :heads].set(bias.astype(F32))
    hh = jnp.arange(heads)
    perm = jnp.zeros((AUX_PIECES * HEAD_DIM, HEAD_DIM), BF16)
    for p in range(AUX_PIECES):
        perm = perm.at[p * HEAD_DIM + hh, AUX_PIECES * hh + p].set(-1.0)
    return pl.pallas_call(
        _fox_forget_kernel,
        out_shape=[jax.ShapeDtypeStruct((s, HEAD_DIM), F32), jax.ShapeDtypeStruct((s, HEAD_DIM), BF16)],
        grid=(s // tm,),
        in_specs=[pl.BlockSpec((tm, d), lambda i: (i, 0)),
                  pl.BlockSpec((d, HEAD_DIM), lambda i: (0, 0)),
                  pl.BlockSpec((1, HEAD_DIM), lambda i: (0, 0)),
                  pl.BlockSpec((tm, tm), lambda i: (0, 0)),
                  pl.BlockSpec((AUX_PIECES * HEAD_DIM, HEAD_DIM), lambda i: (0, 0))],
        out_specs=[pl.BlockSpec((tm, HEAD_DIM), lambda i: (i, 0)),
                   pl.BlockSpec((tm, HEAD_DIM), lambda i: (i, 0))],
        scratch_shapes=[pltpu.VMEM((1, HEAD_DIM), F32)],
        compiler_params=_params("arbitrary"),
        name="fox_forget_cumsum",
    )(n, wf, b, tri, perm)


def _fox_attn_kernel(q_ref, k_ref, v_ref, kaux_ref, c_ref, gate_ref, o_ref,
                     st_a, st_b, cmax_a, cmax_b, p_a, p_b, acc_ref, *, tq, heads_per_step):
    i = pl.program_id(1)
    tk = tq // 2
    lane = lax.broadcasted_iota(jnp.int32, (1, HEAD_DIM), 1)
    hs = range(heads_per_step)

    q_aug = []
    for hh in hs:
        h = pl.program_id(0) * heads_per_step + hh
        c0 = jnp.sum(jnp.where(lane == h, c_ref[0:1, :], 0.0), axis=1, keepdims=True)
        qaux = jnp.where(jnp.logical_and(lane >= AUX_PIECES * h, lane < AUX_PIECES * (h + 1)), 1.0, 0.0)
        for p, piece in enumerate(_split_bf16(c0, AUX_PIECES)):
            qaux = jnp.where(lane == AUX_CONST_LANE + p, piece.astype(F32), qaux)
        q_aug.append(jnp.concatenate(
            [q_ref[hh], jnp.broadcast_to(qaux.astype(BF16), (tq, HEAD_DIM))], axis=1))

    def logits(hh, c):
        row0 = pl.multiple_of(c * tk, tk)
        k_aug = jnp.concatenate([k_ref[hh, pl.ds(row0, tk), :], kaux_ref[pl.ds(row0, tk), :]], axis=1)
        return lax.dot_general(k_aug, q_aug[hh], _NT, preferred_element_type=F32)

    def weighted_values(hh, c, p):
        return jnp.dot(v_ref[hh, c], p, preferred_element_type=F32)

    def fill(hh, c, st_buf, cmax_buf):
        st = logits(hh, c)
        st_buf[hh, :, :tq] = st
        cmax_buf[hh] = jnp.max(st, axis=0, keepdims=True)

    def step(hh, c, m, l, st_cur, cmax_cur, p_prev, p_cur, key0=None):
        acc = acc_ref[hh] + weighted_values(hh, jnp.maximum(c - 1, 0), p_prev[hh, :, :tq])
        st = st_cur[hh, :, :tq]
        if key0 is None:
            cmax = cmax_cur[hh]
            fill(hh, c + 2, st_cur, cmax_cur)
        else:
            keys = lax.broadcasted_iota(jnp.int32, (tk, tq), 0) + key0
            queries = lax.broadcasted_iota(jnp.int32, (tk, tq), 1)
            st = jnp.where(keys <= queries, st, NEG_BIG)
            cmax = jnp.max(st, axis=0, keepdims=True)
        m_new = jnp.maximum(m, cmax)
        alpha = jnp.exp2(m - m_new)
        p = jnp.exp2(st - m_new)
        l_new = alpha * l + jnp.sum(p, axis=0, keepdims=True)
        p_cur[hh, :, :tq] = p.astype(BF16)
        acc_ref[hh] = alpha * acc
        return m_new, l_new

    def steps(c, ml, st_cur, cmax_cur, p_prev, p_cur, key0=None):
        return tuple(step(hh, c, *ml[hh], st_cur, cmax_cur, p_prev, p_cur, key0) for hh in hs)

    for hh in hs:
        fill(hh, 0, st_a, cmax_a)
        fill(hh, 1, st_b, cmax_b)
    p_b[...] = jnp.zeros_like(p_b)
    acc_ref[...] = jnp.zeros_like(acc_ref)

    def pairs(c0, n_pairs, ml):
        for r in range(n_pairs):
            ml = steps(c0 + 2 * r, ml, st_a, cmax_a, p_b, p_a)
            ml = steps(c0 + 2 * r + 1, ml, st_b, cmax_b, p_a, p_b)
        return ml

    ml = tuple((jnp.full((1, tq), NEG_BIG, F32), jnp.zeros((1, tq), F32)) for _ in hs)
    n_trips = i // ATTN_PAIRS_PER_TRIP
    ml = lax.fori_loop(
        0, n_trips, lambda jj, c: pairs(2 * ATTN_PAIRS_PER_TRIP * jj, ATTN_PAIRS_PER_TRIP, c), ml)
    ml = lax.fori_loop(n_trips * ATTN_PAIRS_PER_TRIP, i, lambda jj, c: pairs(2 * jj, 1, c), ml)
    ml = steps(2 * i, ml, st_a, cmax_a, p_b, p_a, key0=0)
    ml = steps(2 * i + 1, ml, st_b, cmax_b, p_a, p_b, key0=tk)
    for hh in hs:
        acc = acc_ref[hh] + weighted_values(hh, 2 * i + 1, p_b[hh, :, :tq])
        o_ref[hh] = ((acc / ml[hh][1]).T * gate_ref[hh].astype(F32)).astype(o_ref.dtype)


def _fox_attention(q, k, v, gate, c, kaux):
    heads, s, _ = q.shape
    tq = _tile(s, ATTN_TQ)
    hps = ATTN_HEADS_PER_STEP if heads % ATTN_HEADS_PER_STEP == 0 else 1
    kern = functools.partial(_fox_attn_kernel, tq=tq, heads_per_step=hps)
    once = pl.Buffered(1)
    kv_spec = lambda: pl.BlockSpec((hps, s, HEAD_DIM), lambda h, i: (h, 0, 0), pipeline_mode=once)
    return pl.pallas_call(
        kern,
        out_shape=jax.ShapeDtypeStruct((heads, s, HEAD_DIM), BF16),
        grid=(heads // hps, s // tq),
        in_specs=[pl.BlockSpec((hps, tq, HEAD_DIM), lambda h, i: (h, i, 0)),
                  kv_spec(),
                  pl.BlockSpec((hps, s // (tq // 2), HEAD_DIM, tq // 2), lambda h, i: (h, 0, 0, 0),
                               pipeline_mode=once),
                  pl.BlockSpec((s, HEAD_DIM), lambda h, i: (0, 0), pipeline_mode=once),
                  pl.BlockSpec((8, HEAD_DIM), lambda h, i: (i * (tq // 8), 0)),
                  pl.BlockSpec((hps, tq, HEAD_DIM), lambda h, i: (h, i, 0))],
        out_specs=pl.BlockSpec((hps, tq, HEAD_DIM), lambda h, i: (h, i, 0)),
        scratch_shapes=[pltpu.VMEM((hps, tq // 2, tq + HEAD_DIM), F32),
                        pltpu.VMEM((hps, tq // 2, tq + HEAD_DIM), F32),
                        pltpu.VMEM((hps, 1, tq), F32), pltpu.VMEM((hps, 1, tq), F32),
                        pltpu.VMEM((hps, tq // 2, tq + HEAD_DIM), BF16),
                        pltpu.VMEM((hps, tq // 2, tq + HEAD_DIM), BF16),
                        pltpu.VMEM((hps, HEAD_DIM, tq), F32)],
        compiler_params=_params("parallel", "arbitrary"),
        name="fox_attention",
    )(q, k, v, kaux, c, gate)


def _hgrn_level_widths(t):
    ws = []
    w = t // 2
    while w >= HGRN_DIAG:
        ws.append(w)
        w //= 2
    return ws


def _hgrn_sum_masks(t):
    r = jnp.arange(t)[:, None]
    c = jnp.arange(t)[None, :]
    blocks = [c <= r]
    for w in _hgrn_level_widths(t):
        ref = (r // (2 * w)) * (2 * w) + w - 1
        q_side = (r % (2 * w)) >= w
        blocks.append(jnp.where(q_side, (c > ref) & (c <= r), (c > r) & (c <= ref)))
    return jnp.concatenate(blocks, axis=0).astype(BF16)


def _hgrn_scan_kernel(q_ref, k_ref, g_ref, v_ref, gate_ref, gain_ref, masks_ref, ones_ref,
                      o_ref, state_ref, *, t, heads_per_step):
    @pl.when(pl.program_id(1) == 0)
    def _():
        state_ref[...] = jnp.zeros_like(state_ref)

    for hh in range(heads_per_step):
        lanes = slice(hh * HEAD_DIM, (hh + 1) * HEAD_DIM)
        _hgrn_scan_head(q_ref.at[:, lanes], k_ref.at[:, lanes], g_ref.at[:, lanes], v_ref.at[:, lanes],
                        gate_ref.at[:, lanes], gain_ref, masks_ref, ones_ref, o_ref.at[:, lanes],
                        state_ref.at[hh], t)


def _hgrn_scan_head(q_ref, k_ref, g_ref, v_ref, gate_ref, gain_ref, masks_ref, ones_ref, o_ref,
                    state_ref, t):
    q = q_ref[...].astype(F32)
    k = k_ref[...].astype(F32)
    v_bf = v_ref[...]
    v = v_bf.astype(F32)
    g = g_ref[...]

    pieces = jnp.concatenate(_split_bf16(g, 2), axis=1)
    sums2 = jnp.dot(masks_ref[...], pieces, preferred_element_type=F32)
    sums = sums2[:, :HEAD_DIM] + sums2[:, HEAD_DIM:]
    b = sums[0:t]
    b_last = b[t - 1:t]

    st = state_ref[...]
    o = lax.dot_general((q * jnp.exp(b)).astype(BF16), st.astype(BF16), _NT, preferred_element_type=F32)

    rows = lax.broadcasted_iota(jnp.int32, (t, 1), 0)
    scores = jnp.zeros((t, t), F32)
    for lvl, w in enumerate(_hgrn_level_widths(t)):
        fac = jnp.exp(sums[(lvl + 1) * t:(lvl + 2) * t])
        q_side = (rows % (2 * w)) >= w
        a = jnp.where(q_side, q * fac, 0.0).astype(BF16)
        bk = jnp.where(q_side, 0.0, k * fac).astype(BF16)
        sc = lax.dot_general(a, bk, _NT, preferred_element_type=F32)
        if 2 * w < t:
            ri = lax.broadcasted_iota(jnp.int32, (t, t), 0) // (2 * w)
            cj = lax.broadcasted_iota(jnp.int32, (t, t), 1) // (2 * w)
            sc = jnp.where(ri == cj, sc, 0.0)
        scores = scores + sc
    o = o + jnp.dot(scores.astype(BF16), v_bf, preferred_element_type=F32)

    def group_roll(a, dd):
        return pltpu.roll(a.reshape(t // HGRN_DIAG, HGRN_DIAG, HEAD_DIM), dd, 1).reshape(t, HEAD_DIM)

    sub = rows % HGRN_DIAG
    rel = jnp.zeros_like(g)
    xs, vds = [q * k], [v]
    for dd in range(1, HGRN_DIAG):
        rel = rel + (group_roll(g, dd - 1) if dd > 1 else g)
        xs.append(q * group_roll(k, dd) * jnp.exp(jnp.where(sub >= dd, rel, NEG_BIG)))
        vds.append(group_roll(v, dd))
    ones2 = ones_ref[...]
    for dd in range(0, HGRN_DIAG, 2):
        pair = jnp.concatenate([xs[dd], xs[dd + 1]], axis=1).astype(BF16)
        rowsums = jnp.dot(pair, ones2, preferred_element_type=F32)
        o = o + rowsums[:, :HEAD_DIM] * vds[dd] + rowsums[:, HEAD_DIM:] * vds[dd + 1]

    kh = (k * jnp.exp(b_last - b)).astype(BF16)
    state_ref[...] = st * jnp.exp(b_last) + lax.dot_general(v_bf, kh, _TN, preferred_element_type=F32)

    ms = jnp.mean(o * o, axis=-1, keepdims=True)
    y = o * lax.rsqrt(ms + NORM_EPS) * gain_ref[...]
    o_ref[...] = (y * gate_ref[...].astype(F32)).astype(o_ref.dtype)


def _hgrn_scan(q, k, g, v, gate, out_gain):
    s, d = g.shape
    heads = d // HEAD_DIM
    t = _tile(s, HGRN_CHUNK)
    masks = _hgrn_sum_masks(t)
    nm = masks.shape[0]
    hps = HGRN_HEADS_PER_STEP if heads % HGRN_HEADS_PER_STEP == 0 else 1
    kern = functools.partial(_hgrn_scan_kernel, t=t, heads_per_step=hps)
    blk = lambda: pl.BlockSpec((t, hps * HEAD_DIM), lambda h, c: (c, h))
    return pl.pallas_call(
        kern,
        out_shape=jax.ShapeDtypeStruct((s, d), BF16),
        grid=(heads // hps, s // t),
        in_specs=[blk(), blk(), blk(), blk(), blk(),
                  pl.BlockSpec((1, HEAD_DIM), lambda h, c: (0, 0)),
                  pl.BlockSpec((nm, t), lambda h, c: (0, 0)),
                  pl.BlockSpec((2 * HEAD_DIM, 2 * HEAD_DIM), lambda h, c: (0, 0))],
        out_specs=blk(),
        scratch_shapes=[pltpu.VMEM((hps, HEAD_DIM, HEAD_DIM), F32)],
        compiler_params=_params("parallel", "arbitrary"),
        name="hgrn_scan",
    )(q, k, g, v, gate, out_gain.reshape(1, HEAD_DIM).astype(F32), masks,
      jnp.kron(jnp.eye(2, dtype=BF16), jnp.ones((HEAD_DIM, HEAD_DIM), BF16)))


def kernel(x, fox_w_in, fox_w_out, fox_q_gain, fox_k_gain, fox_fgate_bias, hgrn_w_in, hgrn_w_out,
           hgrn_out_gain, hgrn_lb_logits, mixer_norm_gain, mlp_norm_gain, mlp_w_up, mlp_w_down,
           final_norm_gain):
    batch, seq, d = x.shape
    depth = mixer_norm_gain.shape[0]
    fox_w_in_b, fox_w_out_b = fox_w_in[:, :, :4 * d].astype(BF16), fox_w_out.astype(BF16)
    fox_w_f_b = fox_w_in[:, :, 4 * d:].astype(BF16)
    hgrn_w_in_b, hgrn_w_out_b = hgrn_w_in.astype(BF16), hgrn_w_out.astype(BF16)
    mlp_w_up_b, mlp_w_down_b = mlp_w_up.astype(BF16), mlp_w_down.astype(BF16)
    q_scale = HEAD_DIM ** -0.5 * LOG2E
    outs = []
    for bi in range(batch):
        h = x[bi]
        for i in range(depth):
            j = i // 2
            n = _rmsnorm(h, mixer_norm_gain[i], BF16)
            if i % 2 == 0:
                q = _in_proj(n, fox_w_in_b, j, 0, "fox_in_proj_q", head_gain=fox_q_gain[j] * q_scale,
                             head_major=True)
                k = _in_proj(n, fox_w_in_b, j, d, "fox_in_proj_k", head_gain=fox_k_gain[j], head_major=True)
                v = _in_proj(n, fox_w_in_b, j, 2 * d, "fox_in_proj_v",
                             transposed_chunk=_tile(seq, ATTN_TQ) // 2)
                gate = _in_proj(n, fox_w_in_b, j, 3 * d, "fox_in_proj_gate", act=_sigmoid, head_major=True)
                c, kaux = _fox_forget_cumsum(n, fox_w_f_b[j], fox_fgate_bias[j])
                mix_in = _fox_attention(q, k, v, gate, c, kaux)
                h = _proj_residual(mix_in, fox_w_out_b, j, h, name="mixer_out_proj")
            else:
                q = _in_proj(n, hgrn_w_in_b, j, 0, "hgrn_in_proj_q", act=_silu)
                g, k = _hgrn_in_proj_forget(n, hgrn_w_in_b, j, hgrn_lb_logits, i)
                v = _in_proj(n, hgrn_w_in_b, j, 2 * d, "hgrn_in_proj_v")
                gate = _in_proj(n, hgrn_w_in_b, j, 3 * d, "hgrn_in_proj_gate", act=_silu)
                mix_in = _hgrn_scan(q, k, g, v, gate, hgrn_out_gain[j])
                h = _proj_residual(mix_in, hgrn_w_out_b, j, h, name="mixer_out_proj")
            n = _rmsnorm(h, mlp_norm_gain[i], BF16)
            a = _mlp_up(n, mlp_w_up_b, i)
            h = _proj_residual(a, mlp_w_down_b, i, h, name="mlp_down")
        outs.append(_rmsnorm(h, final_norm_gain, x.dtype))
    return jnp.stack(outs, axis=0)
```
